```python
import math
import jax, jax.numpy as jnp
from jax import lax
import numpy as np

D_MODEL = 1024
BATCH = 32
SEQ = 2048
DEPTH = 4

D_MIX = D_MODEL
GLA_WIDTH = D_MIX // 2
SG_WIDTH = D_MIX - GLA_WIDTH
GLA_HEADS = 4
GLA_DK = 64
GLA_DV = GLA_WIDTH // GLA_HEADS
GLA_KW = GLA_HEADS * GLA_DK
GATE_RANK = 16
GATE_TAU = 16.0
GLA_CHUNK = 64
SG_GROUPS = 4
SG_GROUP_DIM = SG_WIDTH // SG_GROUPS
SG_CHUNK = 128
EPS = 1e-6
IN_SPLIT = (GLA_KW, GLA_KW, GLA_WIDTH, GATE_RANK, GLA_WIDTH, SG_WIDTH, SG_WIDTH, SG_WIDTH)
IN_WIDTH = sum(IN_SPLIT)

kernel_name = "hymba_style_gla_gmlp_hybrid"


def _rmsnorm(x, g):
    xf = x.astype(jnp.float32)
    y = xf * lax.rsqrt(jnp.mean(xf * xf, axis=-1, keepdims=True) + EPS)
    return (y * g.astype(jnp.float32)).astype(x.dtype)


def _gla(q, k, v, log_a):
    B, S, H, DK = q.shape
    DV = v.shape[-1]
    C = GLA_CHUNK
    N = S // C

    def to_chunks(t):
        return t.astype(jnp.float32).reshape(B, N, C, H, t.shape[-1]).transpose(1, 0, 3, 2, 4)

    qc = to_chunks(q) * (DK ** -0.5)
    kc, vc, gc = to_chunks(k), to_chunks(v), to_chunks(log_a)
    causal = jnp.tril(jnp.ones((C, C), dtype=bool))[:, :, None]

    def step(state, inp):
        qi, ki, vi, gi = inp
        G = jnp.cumsum(gi, axis=-2)
        diff = G[..., :, None, :] - G[..., None, :, :]
        decay = jnp.exp(jnp.where(causal, diff, -jnp.inf))
        scores = jnp.einsum('bhtd,bhtsd,bhsd->bhts', qi, decay, ki)
        o = (jnp.einsum('bhts,bhsv->bhtv', scores, vi)
             + jnp.einsum('bhtd,bhdv->bhtv', qi * jnp.exp(G), state))
        G_last = G[..., -1:, :]
        state = (jnp.exp(G[..., -1, :])[..., None] * state
                 + jnp.einsum('bhsd,bhsv->bhdv', ki * jnp.exp(G_last - G), vi))
        return state, o

    state0 = jnp.zeros((B, H, DK, DV), jnp.float32)
    _, o = lax.scan(step, state0, (qc, kc, vc, gc))
    return o.transpose(1, 0, 3, 2, 4).reshape(B, S, H, DV).astype(v.dtype)


def _spatial_gating(u, v, ln_g, ln_b, w_s, b_s):
    B, S, _ = v.shape
    N = S // SG_CHUNK
    vf = v.astype(jnp.float32).reshape(B, S, SG_GROUPS, SG_GROUP_DIM)
    mu = jnp.mean(vf, axis=-1, keepdims=True)
    var = jnp.mean(jnp.square(vf - mu), axis=-1, keepdims=True)
    vn = (vf - mu) * lax.rsqrt(var + EPS)
    vn = vn * ln_g.astype(jnp.float32).reshape(SG_GROUPS, SG_GROUP_DIM) + ln_b.astype(jnp.float32).reshape(SG_GROUPS, SG_GROUP_DIM)
    vn = vn.reshape(B, N, SG_CHUNK, SG_GROUPS, SG_GROUP_DIM)
    w = w_s.astype(jnp.float32) * jnp.tril(jnp.ones((SG_CHUNK, SG_CHUNK), jnp.float32))[None]
    mixed = jnp.einsum('gts,bnsgc->bntgc', w, vn) + b_s.astype(jnp.float32).T[None, None, :, :, None]
    out = u.astype(jnp.float32).reshape(B, N, SG_CHUNK, SG_GROUPS, SG_GROUP_DIM) * mixed
    return out.reshape(B, S, SG_WIDTH).astype(u.dtype)


def setup_inputs(seed: int = 0) -> dict:
    key = jax.random.key(seed)
    ks = jax.random.split(key, 16)
    f32 = jnp.float32
    nrm = lambda k, shp, s: jax.random.normal(k, shp, f32) * s
    return {
        "x": jax.random.normal(ks[0], (BATCH, SEQ, D_MODEL), f32),
        "c": jax.random.normal(ks[1], (BATCH, D_MODEL), f32),
        "norm_g": 1.0 + nrm(ks[2], (DEPTH, D_MODEL), 0.02),
        "w_ada": nrm(ks[3], (DEPTH, D_MODEL, 3 * D_MODEL), 0.5 * D_MODEL ** -0.5),
        "b_ada": nrm(ks[4], (DEPTH, 3 * D_MODEL), 0.02),
        "w_in": nrm(ks[5], (DEPTH, D_MODEL, IN_WIDTH), D_MODEL ** -0.5),
        "w_alpha2": nrm(ks[6], (DEPTH, GATE_RANK, GLA_KW), GATE_RANK ** -0.5),
        "b_alpha": nrm(ks[7], (DEPTH, GLA_KW), 0.1),
        "gla_norm_g": 1.0 + nrm(ks[8], (DEPTH, GLA_WIDTH), 0.02),
        "sg_ln_g": 1.0 + nrm(ks[9], (DEPTH, SG_WIDTH), 0.02),
        "sg_ln_b": nrm(ks[10], (DEPTH, SG_WIDTH), 0.02),
        "w_spatial": nrm(ks[11], (DEPTH, SG_GROUPS, SG_CHUNK, SG_CHUNK), 0.5 * SG_CHUNK ** -0.5),
        "b_spatial": 1.0 + nrm(ks[12], (DEPTH, SG_GROUPS, SG_CHUNK), 0.01),
        "w_out": nrm(ks[13], (DEPTH, D_MIX, D_MODEL), D_MIX ** -0.5),
        "final_g": 1.0 + nrm(ks[14], (D_MODEL,), 0.02),
    }


def reference(x, c, norm_g, w_ada, b_ada, w_in, w_alpha2, b_alpha, gla_norm_g,
              sg_ln_g, sg_ln_b, w_spatial, b_spatial, w_out, final_g):
    B, S, _ = x.shape
    split_pts = list(np.cumsum(IN_SPLIT)[:-1])
    c_act = jax.nn.silu(c)
    for l in range(DEPTH):
        mod = c_act @ w_ada[l] + b_ada[l]
        shift, scale, gate = jnp.split(mod, 3, axis=-1)
        h = _rmsnorm(x, norm_g[l]) * (1.0 + scale[:, None, :]) + shift[:, None, :]
        p = h @ w_in[l]
        q, k, v_g, a_lr, r_g, u_s, v_s, r_s = jnp.split(p, split_pts, axis=-1)

        z_a = (a_lr @ w_alpha2[l] + b_alpha[l]).astype(jnp.float32)
        log_a = jax.nn.log_sigmoid(z_a) / GATE_TAU
        o_gla = _gla(q.reshape(B, S, GLA_HEADS, GLA_DK),
                     k.reshape(B, S, GLA_HEADS, GLA_DK),
                     v_g.reshape(B, S, GLA_HEADS, GLA_DV),
                     log_a.reshape(B, S, GLA_HEADS, GLA_DK))
        o_gla = _rmsnorm(o_gla, gla_norm_g[l].reshape(GLA_HEADS, GLA_DV)).reshape(B, S, GLA_WIDTH)
        o_gla = o_gla * jax.nn.silu(r_g)

        o_sg = _spatial_gating(jax.nn.gelu(u_s, approximate=False), jax.nn.gelu(v_s, approximate=False),
                               sg_ln_g[l], sg_ln_b[l], w_spatial[l], b_spatial[l])
        o_sg = o_sg * jax.nn.silu(r_s)

        y = jnp.concatenate([o_gla, o_sg], axis=-1) @ w_out[l]
        x = x + gate[:, None, :] * y
    return _rmsnorm(x, final_g)
```

```python
import functools

import jax
import jax.numpy as jnp
from jax import lax
from jax.experimental import pallas as pl
from jax.experimental.pallas import tpu as pltpu

F32 = jnp.float32
BF16 = jnp.bfloat16

D_MODEL = 1024
GLA_HEADS = 4
GLA_DK = 64
GLA_DV = 128
GLA_KW = GLA_HEADS * GLA_DK
GLA_WIDTH = GLA_HEADS * GLA_DV
GATE_RANK = 16
GATE_TAU = 16.0
GLA_CHUNK = 64
SG_GROUPS = 4
SG_GROUP_DIM = 128
SG_WIDTH = SG_GROUPS * SG_GROUP_DIM
SG_CHUNK = 128
EPS = 1e-6

LANES = 128
RANK_PAD = LANES
SEQ_TILE = 256
VMEM_LIMIT_BYTES = 48 * 1024 * 1024

_OFF_Q = 0
_OFF_K = _OFF_Q + GLA_KW
_OFF_A = _OFF_K + GLA_KW
_OFF_VG = _OFF_A + RANK_PAD
_OFF_RG = _OFF_VG + GLA_WIDTH
_OFF_US = _OFF_RG + GLA_WIDTH
_OFF_VS = _OFF_US + SG_WIDTH
_OFF_RS = _OFF_VS + SG_WIDTH
IN_WIDTH_PAD = _OFF_RS + SG_WIDTH


def _split3(a):
    hi = a.astype(BF16)
    r1 = a - hi.astype(F32)
    mid = r1.astype(BF16)
    lo = (r1 - mid.astype(F32)).astype(BF16)
    return hi, mid, lo


def _dot(a, b):
    return jnp.dot(a, b, preferred_element_type=F32)


def _dot_nt(a, b):
    return lax.dot_general(a, b, (((1,), (1,)), ((), ())), preferred_element_type=F32)


def _dot_tn(a, b):
    return lax.dot_general(a, b, (((0,), (0,)), ((), ())), preferred_element_type=F32)


def _silu(a):
    return a * (1.0 / (1.0 + jnp.exp(-a)))


def _gelu(a):
    return 0.5 * a * (1.0 + lax.erf(a * (2.0 ** -0.5)))


def _ada_kernel(c_ref, w_ref, b_ref, o_ref):
    c = c_ref[...]
    ca = _silu(c)
    w = w_ref[0]
    a_hi, a_mid, a_lo = _split3(ca)
    w_hi, w_mid, w_lo = _split3(w)
    acc = _dot(a_hi, w_hi)
    acc += _dot(a_hi, w_mid) + _dot(a_mid, w_hi)
    acc += _dot(a_hi, w_lo) + _dot(a_mid, w_mid) + _dot(a_lo, w_hi)
    o_ref[0] = acc + b_ref[0]


def _ada_call(c, w_ada, b_ada):
    depth, d, d3 = w_ada.shape
    b = c.shape[0]
    nblk = d3 // d
    return pl.pallas_call(
        _ada_kernel,
        grid=(depth, nblk),
        in_specs=[
            pl.BlockSpec((b, d), lambda l, j: (0, 0)),
            pl.BlockSpec((1, d, d), lambda l, j: (l, 0, j)),
            pl.BlockSpec((1, 1, d), lambda l, j: (l, 0, j)),
        ],
        out_specs=pl.BlockSpec((1, b, d), lambda l, j: (l, 0, j)),
        out_shape=jax.ShapeDtypeStruct((depth, b, d3), F32),
        compiler_params=pltpu.CompilerParams(
            dimension_semantics=("arbitrary", "arbitrary"),
            vmem_limit_bytes=VMEM_LIMIT_BYTES),
        name="adaln_mod",
    )(c, w_ada, b_ada.reshape(depth, 1, d3))


def _layer_kernel(x_ref, mod_ref, ng_ref, win_ref, wa2_ref, ba_ref, gng_ref, lng_ref, lnb_ref,
                  ws_ref, bsp_ref, wout_ref, fg_ref, o_ref, st_ref, yin_ref, *, final):
    ts = x_ref.shape[1]
    n_gla_chunks = ts // GLA_CHUNK
    n_sg_chunks = ts // SG_CHUNK

    @pl.when(pl.program_id(1) == 0)
    def _():
        st_ref[...] = jnp.zeros_like(st_ref)

    x = x_ref[0]
    shift = mod_ref[0, 0:1, :]
    scale = mod_ref[0, 1:2, :]
    gate = mod_ref[0, 2:3, :]

    ms = jnp.mean(x * x, axis=-1, keepdims=True)
    h = (x * lax.rsqrt(ms + EPS)) * (ng_ref[...] * (1.0 + scale)) + shift
    hb = h.astype(BF16)

    def proj(off, width):
        return _dot(hb, win_ref[:, off:off + width])

    q = proj(_OFF_Q, GLA_KW)
    k = proj(_OFF_K, GLA_KW)
    a_lr = proj(_OFF_A, RANK_PAD)
    z = _dot(a_lr.astype(BF16), wa2_ref[...]) + ba_ref[...]
    log_a = (jnp.minimum(z, 0.0) - jnp.log1p(jnp.exp(-jnp.abs(z)))) * (1.0 / GATE_TAU)

    row = lax.broadcasted_iota(jnp.int32, (ts, ts), 0)
    col = lax.broadcasted_iota(jnp.int32, (ts, ts), 1)
    same_chunk = (row // GLA_CHUNK) == (col // GLA_CHUNK)
    cum_mat = jnp.where(same_chunk & (col <= row), 1.0, 0.0).astype(BF16)
    tot_mat = jnp.where(same_chunk, 1.0, 0.0).astype(BF16)
    la_hi, la_mid, la_lo = _split3(log_a)
    g_cum = _dot(cum_mat, la_hi) + _dot(cum_mat, la_mid) + _dot(cum_mat, la_lo)
    g_tot = _dot(tot_mat, la_hi) + _dot(tot_mat, la_mid) + _dot(tot_mat, la_lo)

    q_t = (q * jnp.exp(g_cum) * (GLA_DK ** -0.5)).astype(BF16)
    k_t = (k * jnp.exp(-g_cum)).astype(BF16)
    k_h = (k * jnp.exp(g_tot - g_cum)).astype(BF16)
    dec = jnp.exp(g_tot)

    v_g = proj(_OFF_VG, GLA_WIDTH).astype(BF16)

    lane = lax.broadcasted_iota(jnp.int32, (GLA_CHUNK, LANES), 1)
    head_mask = (lane < GLA_DK, lane >= GLA_DK)
    ct = lax.broadcasted_iota(jnp.int32, (GLA_CHUNK, GLA_CHUNK), 0)
    cs = lax.broadcasted_iota(jnp.int32, (GLA_CHUNK, GLA_CHUNK), 1)
    causal = cs <= ct

    o_chunks = []
    state = [st_ref[0], st_ref[1]]
    for c in range(n_gla_chunks):
        r0 = c * GLA_CHUNK
        o_heads = [None] * GLA_HEADS
        for p in range(GLA_HEADS // 2):
            c0 = p * LANES
            q_p = q_t[r0:r0 + GLA_CHUNK, c0:c0 + LANES]
            k_p = k_t[r0:r0 + GLA_CHUNK, c0:c0 + LANES]
            kh_p = k_h[r0:r0 + GLA_CHUNK, c0:c0 + LANES]
            st_b = state[p].astype(BF16)
            v_rows = []
            kh_rows = []
            for hh in range(2):
                hd = 2 * p + hh
                q_m = jnp.where(head_mask[hh], q_p, jnp.zeros_like(q_p))
                att = _dot_nt(q_m, k_p)
                att = jnp.where(causal, att, 0.0).astype(BF16)
                v_h = v_g[r0:r0 + GLA_CHUNK, hd * GLA_DV:(hd + 1) * GLA_DV]
                o_heads[hd] = _dot(att, v_h) + _dot_nt(q_m, st_b)
                v_rows.append(v_h)
                kh_rows.append(jnp.where(head_mask[hh], kh_p, jnp.zeros_like(kh_p)))
            upd = _dot_tn(jnp.concatenate(v_rows, axis=0), jnp.concatenate(kh_rows, axis=0))
            state[p] = dec[r0:r0 + 1, c0:c0 + LANES] * state[p] + upd
        o_chunks.append(jnp.concatenate(o_heads, axis=1))
    st_ref[0] = state[0]
    st_ref[1] = state[1]
    o_gla = jnp.concatenate(o_chunks, axis=0)

    r_g = proj(_OFF_RG, GLA_WIDTH)
    gng = gng_ref[...]
    for hd in range(GLA_HEADS):
        sl = slice(hd * GLA_DV, (hd + 1) * GLA_DV)
        o_h = o_gla[:, sl]
        o_h = o_h * lax.rsqrt(jnp.mean(o_h * o_h, axis=-1, keepdims=True) + EPS) * gng[:, sl]
        yin_ref[:, sl] = (o_h * _silu(r_g[:, sl])).astype(BF16)

    u_s = _gelu(proj(_OFF_US, SG_WIDTH))
    v_s = _gelu(proj(_OFF_VS, SG_WIDTH))
    r_s = proj(_OFF_RS, SG_WIDTH)
    lng = lng_ref[...]
    lnb = lnb_ref[...]
    st_ = lax.broadcasted_iota(jnp.int32, (SG_CHUNK, SG_CHUNK), 0)
    ss_ = lax.broadcasted_iota(jnp.int32, (SG_CHUNK, SG_CHUNK), 1)
    sg_causal = ss_ <= st_
    for g in range(SG_GROUPS):
        sl = slice(g * SG_GROUP_DIM, (g + 1) * SG_GROUP_DIM)
        v_grp = v_s[:, sl]
        mu = jnp.mean(v_grp, axis=-1, keepdims=True)
        cen = v_grp - mu
        var = jnp.mean(cen * cen, axis=-1, keepdims=True)
        vn = (cen * lax.rsqrt(var + EPS) * lng[:, sl] + lnb[:, sl]).astype(BF16)
        w_g = jnp.where(sg_causal, ws_ref[g], 0.0).astype(BF16)
        vn_wide = jnp.concatenate(
            [vn[n * SG_CHUNK:(n + 1) * SG_CHUNK, :] for n in range(n_sg_chunks)], axis=1)
        mixed_wide = _dot(w_g, vn_wide)
        bias = bsp_ref[:, sl]
        for n in range(n_sg_chunks):
            rs_ = slice(n * SG_CHUNK, (n + 1) * SG_CHUNK)
            mixed = mixed_wide[:, n * SG_GROUP_DIM:(n + 1) * SG_GROUP_DIM] + bias
            out = u_s[rs_, sl] * mixed * _silu(r_s[rs_, sl])
            yin_ref[rs_, GLA_WIDTH + g * SG_GROUP_DIM:GLA_WIDTH + (g + 1) * SG_GROUP_DIM] = (
                out.astype(BF16))

    y = _dot(yin_ref[...], wout_ref[...])
    x_new = x + gate * y
    if final:
        ms2 = jnp.mean(x_new * x_new, axis=-1, keepdims=True)
        x_new = x_new * lax.rsqrt(ms2 + EPS) * fg_ref[...]
    o_ref[0] = x_new


def _layer_call(x, mod, ng, win, wa2, ba, gng, lng, lnb, ws, bsp, wout, fg, *, final):
    b, s, d = x.shape
    ts = SEQ_TILE
    const2 = lambda i, j: (0, 0)
    const3 = lambda i, j: (0, 0, 0)
    return pl.pallas_call(
        functools.partial(_layer_kernel, final=final),
        grid=(b, s // ts),
        in_specs=[
            pl.BlockSpec((1, ts, d), lambda i, j: (i, j, 0)),
            pl.BlockSpec((1, 3, d), lambda i, j: (i, 0, 0)),
            pl.BlockSpec((1, d), const2),
            pl.BlockSpec((d, IN_WIDTH_PAD), const2),
            pl.BlockSpec((RANK_PAD, GLA_KW), const2),
            pl.BlockSpec((1, GLA_KW), const2),
            pl.BlockSpec((1, GLA_WIDTH), const2),
            pl.BlockSpec((1, SG_WIDTH), const2),
            pl.BlockSpec((1, SG_WIDTH), const2),
            pl.BlockSpec((SG_GROUPS, SG_CHUNK, SG_CHUNK), const3),
            pl.BlockSpec((SG_CHUNK, SG_WIDTH), const2),
            pl.BlockSpec((d, d), const2),
            pl.BlockSpec((1, d), const2),
        ],
        out_specs=pl.BlockSpec((1, ts, d), lambda i, j: (i, j, 0)),
        out_shape=jax.ShapeDtypeStruct((b, s, d), F32),
        scratch_shapes=[
            pltpu.VMEM((GLA_HEADS // 2, GLA_DV, LANES), F32),
            pltpu.VMEM((ts, d), BF16),
        ],
        compiler_params=pltpu.CompilerParams(
            dimension_semantics=("arbitrary", "arbitrary"),
            vmem_limit_bytes=VMEM_LIMIT_BYTES),
        name="gla_gmlp_layer",
    )(x, mod, ng, win, wa2, ba, gng, lng, lnb, ws, bsp, wout, fg)


def kernel(x, c, norm_g, w_ada, b_ada, w_in, w_alpha2, b_alpha, gla_norm_g, sg_ln_g, sg_ln_b,
           w_spatial, b_spatial, w_out, final_g):
    depth = w_in.shape[0]
    bsz = x.shape[0]
    d = x.shape[-1]

    mod = _ada_call(c, w_ada, b_ada)
    mod = mod.reshape(depth, bsz, 3, d)

    o_q, o_k, o_vg = 0, GLA_KW, 2 * GLA_KW
    o_a = o_vg + GLA_WIDTH
    o_rest = o_a + GATE_RANK
    w_a_pad = jnp.pad(w_in[:, :, o_a:o_rest], ((0, 0), (0, 0), (0, RANK_PAD - GATE_RANK)))
    win = jnp.concatenate(
        [w_in[:, :, o_q:o_vg], w_a_pad, w_in[:, :, o_vg:o_a], w_in[:, :, o_rest:]],
        axis=-1).astype(BF16)
    wa2 = jnp.pad(w_alpha2, ((0, 0), (0, RANK_PAD - GATE_RANK), (0, 0))).astype(BF16)
    wout = w_out.astype(BF16)
    bsp = jnp.repeat(jnp.transpose(b_spatial, (0, 2, 1)), SG_GROUP_DIM, axis=-1)

    for l in range(depth):
        x = _layer_call(
            x, mod[l], norm_g[l][None], win[l], wa2[l], b_alpha[l][None], gla_norm_g[l][None],
            sg_ln_g[l][None], sg_ln_b[l][None], w_spatial[l], bsp[l], wout[l], final_g[None],
            final=(l == depth - 1))
    return x
```

```python
import functools

import jax
import jax.numpy as jnp
from jax import lax
from jax.experimental import pallas as pl
from jax.experimental.pallas import tpu as pltpu

F32 = jnp.float32
BF16 = jnp.bfloat16

D_MODEL = 1024
GLA_HEADS = 4
GLA_DK = 64
GLA_DV = 128
GLA_KW = GLA_HEADS * GLA_DK
GLA_WIDTH = GLA_HEADS * GLA_DV
GATE_RANK = 16
GATE_TAU = 16.0
GLA_CHUNK = 64
SG_GROUPS = 4
SG_GROUP_DIM = 128
SG_WIDTH = SG_GROUPS * SG_GROUP_DIM
SG_CHUNK = 128
EPS = 1e-6

LANES = 128
RANK_PAD = LANES
SEQ_TILE = 512
CUM_BLOCK = 256
VMEM_LIMIT_BYTES = 48 * 1024 * 1024

_OFF_Q = 0
_OFF_K = _OFF_Q + GLA_KW
_OFF_A = _OFF_K + GLA_KW
_OFF_VG = _OFF_A + RANK_PAD
_OFF_RG = _OFF_VG + GLA_WIDTH
_OFF_US = _OFF_RG + GLA_WIDTH
_OFF_VS = _OFF_US + SG_WIDTH
_OFF_RS = _OFF_VS + SG_WIDTH
IN_WIDTH_PAD = _OFF_RS + SG_WIDTH


def _split3(a):
    hi = a.astype(BF16)
    r1 = a - hi.astype(F32)
    mid = r1.astype(BF16)
    lo = (r1 - mid.astype(F32)).astype(BF16)
    return hi, mid, lo


def _dot(a, b):
    return jnp.dot(a, b, preferred_element_type=F32)


def _dot_nt(a, b):
    return lax.dot_general(a, b, (((1,), (1,)), ((), ())), preferred_element_type=F32)


def _dot_tn(a, b):
    return lax.dot_general(a, b, (((0,), (0,)), ((), ())), preferred_element_type=F32)


def _silu(a):
    return a * (1.0 / (1.0 + jnp.exp(-a)))


def _gelu(a):
    return 0.5 * a * (1.0 + lax.erf(a * (2.0 ** -0.5)))


def _ada_kernel(c_ref, w_ref, b_ref, o_ref):
    c = c_ref[...]
    ca = _silu(c)
    w = w_ref[0]
    a_hi, a_mid, a_lo = _split3(ca)
    w_hi, w_mid, w_lo = _split3(w)
    acc = _dot(a_hi, w_hi)
    acc += _dot(a_hi, w_mid) + _dot(a_mid, w_hi)
    acc += _dot(a_hi, w_lo) + _dot(a_mid, w_mid) + _dot(a_lo, w_hi)
    o_ref[0] = acc + b_ref[0]


def _ada_call(c, w_ada, b_ada):
    depth, d, d3 = w_ada.shape
    b = c.shape[0]
    nblk = d3 // d
    return pl.pallas_call(
        _ada_kernel,
        grid=(depth, nblk),
        in_specs=[
            pl.BlockSpec((b, d), lambda l, j: (0, 0)),
            pl.BlockSpec((1, d, d), lambda l, j: (l, 0, j)),
            pl.BlockSpec((1, 1, d), lambda l, j: (l, 0, j)),
        ],
        out_specs=pl.BlockSpec((1, b, d), lambda l, j: (l, 0, j)),
        out_shape=jax.ShapeDtypeStruct((depth, b, d3), F32),
        compiler_params=pltpu.CompilerParams(
            dimension_semantics=("arbitrary", "arbitrary"),
            vmem_limit_bytes=VMEM_LIMIT_BYTES),
        name="adaln_mod",
    )(c, w_ada, b_ada.reshape(depth, 1, d3))


def _layer_kernel(x_ref, mod_ref, ng_ref, win_ref, wa2_ref, ba_ref, gng_ref, lng_ref, lnb_ref,
                  ws_ref, bsp_ref, wout_ref, fg_ref, o_ref, st_ref, yin_ref, *, final):
    ts = x_ref.shape[1]
    n_gla_chunks = ts // GLA_CHUNK
    n_sg_chunks = ts // SG_CHUNK

    @pl.when(pl.program_id(1) == 0)
    def _():
        st_ref[...] = jnp.zeros_like(st_ref)

    x = x_ref[0]
    shift = mod_ref[0, 0:1, :]
    scale = mod_ref[0, 1:2, :]
    gate = mod_ref[0, 2:3, :]

    ms = jnp.mean(x * x, axis=-1, keepdims=True)
    h = (x * lax.rsqrt(ms + EPS)) * (ng_ref[...] * (1.0 + scale)) + shift
    hb = h.astype(BF16)

    def proj(off, width):
        return _dot(hb, win_ref[:, off:off + width])

    gng = gng_ref[...]
    lng = lng_ref[...]
    lnb = lnb_ref[...]

    half = ts // 2
    w_a = win_ref[:, _OFF_A:_OFF_A + RANK_PAD]
    a_lr = jnp.concatenate([_dot(hb[:half], w_a), _dot(hb[half:], w_a)], axis=0)
    z = _dot(a_lr.astype(BF16), wa2_ref[...]) + ba_ref[...]
    qk = proj(_OFF_Q, 2 * GLA_KW)
    q = qk[:, :GLA_KW]
    k = qk[:, GLA_KW:]
    log_a = (jnp.minimum(z, 0.0) - jnp.log1p(jnp.exp(-jnp.abs(z)))) * (1.0 / GATE_TAU)
    v_g = proj(_OFF_VG, GLA_WIDTH).astype(BF16)

    row = lax.broadcasted_iota(jnp.int32, (CUM_BLOCK, CUM_BLOCK), 0)
    col = lax.broadcasted_iota(jnp.int32, (CUM_BLOCK, CUM_BLOCK), 1)
    same_chunk = (row // GLA_CHUNK) == (col // GLA_CHUNK)
    cum_mat = jnp.where(same_chunk & (col <= row), 1.0, 0.0).astype(BF16)
    la_parts = _split3(log_a)
    g_cum = jnp.concatenate(
        [sum(_dot(cum_mat, part[r:r + CUM_BLOCK]) for part in la_parts)
         for r in range(0, ts, CUM_BLOCK)], axis=0)
    r_g = _silu(proj(_OFF_RG, GLA_WIDTH))

    q_t = (q * jnp.exp(g_cum) * (GLA_DK ** -0.5)).astype(BF16)
    k_t = (k * jnp.exp(-g_cum)).astype(BF16)

    pr = lax.broadcasted_iota(jnp.int32, (2 * GLA_CHUNK, LANES), 0)
    pc = lax.broadcasted_iota(jnp.int32, (2 * GLA_CHUNK, LANES), 1)
    own_lanes = (pr // GLA_CHUNK) == (pc // GLA_DK)
    pair_causal = own_lanes & ((pc % GLA_CHUNK) <= (pr % GLA_CHUNK))

    def stack_heads(a):
        a2 = jnp.concatenate([a, a], axis=0)
        return jnp.where(own_lanes, a2, jnp.zeros_like(a2))

    state = [st_ref[0], st_ref[1]]
    o_rows = [[None] * GLA_HEADS for _ in range(n_gla_chunks)]

    def gla_chunk(c):
        r0 = c * GLA_CHUNK
        g_last = g_cum[r0 + GLA_CHUNK - 1:r0 + GLA_CHUNK, :]
        k_h = (k[r0:r0 + GLA_CHUNK] * jnp.exp(g_last - g_cum[r0:r0 + GLA_CHUNK])).astype(BF16)
        dec = jnp.exp(g_last)
        for p in range(GLA_HEADS // 2):
            c0 = p * LANES
            q_s = stack_heads(q_t[r0:r0 + GLA_CHUNK, c0:c0 + LANES])
            k_s = stack_heads(k_t[r0:r0 + GLA_CHUNK, c0:c0 + LANES])
            kh_s = stack_heads(k_h[:, c0:c0 + LANES])
            v_s2 = jnp.concatenate(
                [v_g[r0:r0 + GLA_CHUNK, (2 * p + hh) * GLA_DV:(2 * p + hh + 1) * GLA_DV]
                 for hh in range(2)], axis=0)
            att = _dot_nt(q_s, k_s)
            att = jnp.where(pair_causal, att, 0.0).astype(BF16)
            o_pair = _dot(att, v_s2) + _dot_nt(q_s, state[p].astype(BF16))
            o_rows[c][2 * p] = o_pair[:GLA_CHUNK]
            o_rows[c][2 * p + 1] = o_pair[GLA_CHUNK:]
            state[p] = dec[:, c0:c0 + LANES] * state[p] + _dot_tn(v_s2, kh_s)

    st_ = lax.broadcasted_iota(jnp.int32, (SG_CHUNK, SG_CHUNK), 0)
    ss_ = lax.broadcasted_iota(jnp.int32, (SG_CHUNK, SG_CHUNK), 1)
    sg_causal = ss_ <= st_

    def sg_mix(v_act, g):
        sl = slice(g * SG_GROUP_DIM, (g + 1) * SG_GROUP_DIM)
        v_grp = v_act[:, sl]
        mu = jnp.mean(v_grp, axis=-1, keepdims=True)
        cen = v_grp - mu
        var = jnp.mean(cen * cen, axis=-1, keepdims=True)
        vn = (cen * lax.rsqrt(var + EPS) * lng[:, sl] + lnb[:, sl]).astype(BF16)
        w_g = jnp.where(sg_causal, ws_ref[g], 0.0).astype(BF16)
        vn_wide = jnp.concatenate(
            [vn[n * SG_CHUNK:(n + 1) * SG_CHUNK, :] for n in range(n_sg_chunks)], axis=1)
        return _dot(w_g, vn_wide)

    def sg_out(mixed_wide, u_act, r_act, g):
        sl = slice(g * SG_GROUP_DIM, (g + 1) * SG_GROUP_DIM)
        bias = bsp_ref[:, sl]
        for n in range(n_sg_chunks):
            rs_ = slice(n * SG_CHUNK, (n + 1) * SG_CHUNK)
            mixed = mixed_wide[:, n * SG_GROUP_DIM:(n + 1) * SG_GROUP_DIM] + bias
            out = u_act[rs_, sl] * mixed * r_act[rs_, sl]
            yin_ref[rs_, GLA_WIDTH + g * SG_GROUP_DIM:GLA_WIDTH + (g + 1) * SG_GROUP_DIM] = (
                out.astype(BF16))

    acts = {}
    fillers = [
        lambda: acts.__setitem__("v", _gelu(proj(_OFF_VS, SG_WIDTH))),
        lambda: acts.__setitem__("u", _gelu(proj(_OFF_US, SG_WIDTH))),
        lambda: acts.__setitem__("r", _silu(proj(_OFF_RS, SG_WIDTH))),
    ] + [
        functools.partial(
            lambda g: sg_out(sg_mix(acts["v"], g), acts["u"], acts["r"], g), g)
        for g in range(SG_GROUPS)
    ]
    for c in range(n_gla_chunks):
        if fillers:
            fillers.pop(0)()
        gla_chunk(c)
    for f in fillers:
        f()
    st_ref[0] = state[0]
    st_ref[1] = state[1]

    for hd in range(GLA_HEADS):
        sl = slice(hd * GLA_DV, (hd + 1) * GLA_DV)
        o_h = jnp.concatenate([o_rows[c][hd] for c in range(n_gla_chunks)], axis=0)
        o_h = o_h * lax.rsqrt(jnp.mean(o_h * o_h, axis=-1, keepdims=True) + EPS) * gng[:, sl]
        yin_ref[:, sl] = (o_h * r_g[:, sl]).astype(BF16)

    y = _dot(yin_ref[...], wout_ref[...])
    x_new = x + gate * y
    if final:
        ms2 = jnp.mean(x_new * x_new, axis=-1, keepdims=True)
        x_new = x_new * lax.rsqrt(ms2 + EPS) * fg_ref[...]
    o_ref[0] = x_new


def _layer_call(x, mod, ng, win, wa2, ba, gng, lng, lnb, ws, bsp, wout, fg, *, final):
    b, s, d = x.shape
    ts = SEQ_TILE
    const2 = lambda i, j: (0, 0)
    const3 = lambda i, j: (0, 0, 0)
    return pl.pallas_call(
        functools.partial(_layer_kernel, final=final),
        grid=(b, s // ts),
        in_specs=[
            pl.BlockSpec((1, ts, d), lambda i, j: (i, j, 0)),
            pl.BlockSpec((1, 3, d), lambda i, j: (i, 0, 0)),
            pl.BlockSpec((1, d), const2),
            pl.BlockSpec((d, IN_WIDTH_PAD), const2),
            pl.BlockSpec((RANK_PAD, GLA_KW), const2),
            pl.BlockSpec((1, GLA_KW), const2),
            pl.BlockSpec((1, GLA_WIDTH), const2),
            pl.BlockSpec((1, SG_WIDTH), const2),
            pl.BlockSpec((1, SG_WIDTH), const2),
            pl.BlockSpec((SG_GROUPS, SG_CHUNK, SG_CHUNK), const3),
            pl.BlockSpec((SG_CHUNK, SG_WIDTH), const2),
            pl.BlockSpec((d, d), const2),
            pl.BlockSpec((1, d), const2),
        ],
        out_specs=pl.BlockSpec((1, ts, d), lambda i, j: (i, j, 0)),
        out_shape=jax.ShapeDtypeStruct((b, s, d), F32),
        scratch_shapes=[
            pltpu.VMEM((GLA_HEADS // 2, GLA_DV, LANES), F32),
            pltpu.VMEM((ts, d), BF16),
        ],
        compiler_params=pltpu.CompilerParams(
            dimension_semantics=("arbitrary", "arbitrary"),
            vmem_limit_bytes=VMEM_LIMIT_BYTES),
        name="gla_gmlp_layer",
    )(x, mod, ng, win, wa2, ba, gng, lng, lnb, ws, bsp, wout, fg)


def kernel(x, c, norm_g, w_ada, b_ada, w_in, w_alpha2, b_alpha, gla_norm_g, sg_ln_g, sg_ln_b,
           w_spatial, b_spatial, w_out, final_g):
    depth = w_in.shape[0]
    bsz = x.shape[0]
    d = x.shape[-1]

    mod = _ada_call(c, w_ada, b_ada)
    mod = mod.reshape(depth, bsz, 3, d)

    o_q, o_k, o_vg = 0, GLA_KW, 2 * GLA_KW
    o_a = o_vg + GLA_WIDTH
    o_rest = o_a + GATE_RANK
    w_a_pad = jnp.pad(w_in[:, :, o_a:o_rest], ((0, 0), (0, 0), (0, RANK_PAD - GATE_RANK)))
    win = jnp.concatenate(
        [w_in[:, :, o_q:o_vg], w_a_pad, w_in[:, :, o_vg:o_a], w_in[:, :, o_rest:]],
        axis=-1).astype(BF16)
    wa2 = jnp.pad(w_alpha2, ((0, 0), (0, RANK_PAD - GATE_RANK), (0, 0))).astype(BF16)
    wout = w_out.astype(BF16)
    bsp = jnp.repeat(jnp.transpose(b_spatial, (0, 2, 1)), SG_GROUP_DIM, axis=-1)

    for l in range(depth):
        x = _layer_call(
            x, mod[l], norm_g[l][None], win[l], wa2[l], b_alpha[l][None], gla_norm_g[l][None],
            sg_ln_g[l][None], sg_ln_b[l][None], w_spatial[l], bsp[l], wout[l], final_g[None],
            final=(l == depth - 1))
    return x
```

```python
import functools

import jax
import jax.numpy as jnp
from jax import lax
from jax.experimental import pallas as pl
from jax.experimental.pallas import tpu as pltpu

F32 = jnp.float32
BF16 = jnp.bfloat16

D_MODEL = 1024
GLA_HEADS = 4
GLA_DK = 64
GLA_DV = 128
GLA_KW = GLA_HEADS * GLA_DK
GLA_WIDTH = GLA_HEADS * GLA_DV
GATE_RANK = 16
GATE_TAU = 16.0
GLA_CHUNK = 64
SG_GROUPS = 4
SG_GROUP_DIM = 128
SG_WIDTH = SG_GROUPS * SG_GROUP_DIM
SG_CHUNK = 128
EPS = 1e-6

LANES = 128
MXU_DIM = 256
RANK_PAD = LANES
SEQ_TILE = 1024
VMEM_LIMIT_BYTES = 48 * 1024 * 1024

_OFF_Q = 0
_OFF_K = _OFF_Q + GLA_KW
_OFF_A = _OFF_K + GLA_KW
_OFF_VG = _OFF_A + RANK_PAD
_OFF_RG = _OFF_VG + GLA_WIDTH
_OFF_US = _OFF_RG + GLA_WIDTH
_OFF_VS = _OFF_US + SG_WIDTH
_OFF_RS = _OFF_VS + SG_WIDTH
IN_WIDTH_PAD = _OFF_RS + SG_WIDTH


def _split3(a):
    hi = a.astype(BF16)
    r1 = a - hi.astype(F32)
    mid = r1.astype(BF16)
    lo = (r1 - mid.astype(F32)).astype(BF16)
    return hi, mid, lo


def _split2(a):
    hi = a.astype(BF16)
    return hi, (a - hi.astype(F32)).astype(BF16)


def _dot(a, b):
    return jnp.dot(a, b, preferred_element_type=F32)


def _dot_nt(a, b):
    return lax.dot_general(a, b, (((1,), (1,)), ((), ())), preferred_element_type=F32)


def _dot_tn(a, b):
    return lax.dot_general(a, b, (((0,), (0,)), ((), ())), preferred_element_type=F32)


def _silu(a):
    return a * (1.0 / (1.0 + jnp.exp(-a)))


def _gelu(a):
    return 0.5 * a * (1.0 + lax.erf(a * (2.0 ** -0.5)))


def _ada_kernel(c_ref, w_ref, b_ref, o_ref):
    c = c_ref[...]
    ca = _silu(c)
    w = w_ref[0]
    a_hi, a_mid, a_lo = _split3(ca)
    w_hi, w_mid, w_lo = _split3(w)
    acc = _dot(a_hi, w_hi)
    acc += _dot(a_hi, w_mid) + _dot(a_mid, w_hi)
    acc += _dot(a_hi, w_lo) + _dot(a_mid, w_mid) + _dot(a_lo, w_hi)
    o_ref[0] = acc + b_ref[0]


def _ada_call(c, w_ada, b_ada):
    depth, d, d3 = w_ada.shape
    b = c.shape[0]
    nblk = d3 // d
    return pl.pallas_call(
        _ada_kernel,
        grid=(depth, nblk),
        in_specs=[
            pl.BlockSpec((b, d), lambda l, j: (0, 0)),
            pl.BlockSpec((1, d, d), lambda l, j: (l, 0, j)),
            pl.BlockSpec((1, 1, d), lambda l, j: (l, 0, j)),
        ],
        out_specs=pl.BlockSpec((1, b, d), lambda l, j: (l, 0, j)),
        out_shape=jax.ShapeDtypeStruct((depth, b, d3), F32),
        compiler_params=pltpu.CompilerParams(
            dimension_semantics=("arbitrary", "arbitrary"),
            vmem_limit_bytes=VMEM_LIMIT_BYTES),
        name="adaln_mod",
    )(c, w_ada, b_ada.reshape(depth, 1, d3))


def _layer_kernel(xc_ref, modc_ref, ng_ref, win_ref, wa2_ref, ba_ref, gng_ref,
                  lng_ref, lnb_ref, ws_ref, bsp_ref, wout_ref, fg_ref, o_ref,
                  st_ref, yin_ref, *, final, tiles_per_seq):
    g = pl.program_id(0)
    ts = xc_ref.shape[1]
    n_gla_chunks = ts // GLA_CHUNK
    n_sg_chunks = ts // SG_CHUNK
    half_rows = ts // 2

    @pl.when(g % tiles_per_seq == 0)
    def _():
        st_ref[...] = jnp.zeros_like(st_ref)

    xx = xc_ref[0]
    shift = modc_ref[0, 0:1, :]
    scale = modc_ref[0, 1:2, :]
    ms = jnp.mean(xx * xx, axis=-1, keepdims=True)
    hb = ((xx * lax.rsqrt(ms + EPS)) * (ng_ref[...] * (1.0 + scale)) + shift).astype(BF16)

    def proj(off, width, rows=slice(None)):
        return _dot(hb[rows], win_ref[:, off:off + width])

    gng = gng_ref[...]
    lng = lng_ref[...]
    lnb = lnb_ref[...]


    a_lr = jnp.concatenate(
        [proj(_OFF_A, RANK_PAD, slice(0, half_rows)), proj(_OFF_A, RANK_PAD, slice(half_rows, ts))],
        axis=0)
    z = _dot(a_lr.astype(BF16), wa2_ref[...]) + ba_ref[...]
    qk = proj(_OFF_Q, 2 * GLA_KW)
    q = qk[:, :GLA_KW]
    k = qk[:, GLA_KW:]
    log_a = (jnp.minimum(z, 0.0) - jnp.log(1.0 + jnp.exp(-jnp.abs(z)))) * (1.0 / GATE_TAU)
    v_g = proj(_OFF_VG, GLA_WIDTH).astype(BF16)

    row = lax.broadcasted_iota(jnp.int32, (MXU_DIM, MXU_DIM), 0)
    col = lax.broadcasted_iota(jnp.int32, (MXU_DIM, MXU_DIM), 1)
    same_chunk = (row // GLA_CHUNK) == (col // GLA_CHUNK)
    cum_mat = jnp.where(same_chunk & (col <= row), 1.0, 0.0).astype(BF16)
    la_parts = _split2(log_a)
    g_cum = jnp.concatenate(
        [sum(_dot(cum_mat, part[r:r + MXU_DIM]) for part in la_parts)
         for r in range(0, ts, MXU_DIM)], axis=0)
    r_g =_silu(proj(_OFF_RG, GLA_WIDTH))

    q_t =(q * jnp.exp(g_cum) * (GLA_DK ** -0.5)).astype(BF16)
    k_t = (k * jnp.exp(-g_cum)).astype(BF16)

    pr = lax.broadcasted_iota(jnp.int32, (2 * GLA_CHUNK, LANES), 0)
    pc = lax.broadcasted_iota(jnp.int32, (2 * GLA_CHUNK, LANES), 1)
    own_lanes = (pr // GLA_CHUNK) == (pc // GLA_DK)
    pair_causal = own_lanes & ((pc % GLA_CHUNK) <= (pr % GLA_CHUNK))
    pairs = range(GLA_HEADS // 2)

    def stack_heads(a):
        a2 = jnp.concatenate([a, a], axis=0)
        return jnp.where(own_lanes, a2, jnp.zeros_like(a2))

    state = [st_ref[p] for p in pairs]
    q_stack = {}
    att = {}
    o_rows = [[None] * GLA_HEADS for _ in range(n_gla_chunks)]

    def gla_scores(c):
        rows = slice(c * GLA_CHUNK, (c + 1) * GLA_CHUNK)
        for p in pairs:
            lanes = slice(p * LANES, (p + 1) * LANES)
            q_stack[c, p] = stack_heads(q_t[rows, lanes])
            s = _dot_nt(q_stack[c, p], stack_heads(k_t[rows, lanes]))
            att[c, p] = jnp.where(pair_causal, s, 0.0).astype(BF16)

    def gla_apply(c):
        r0 = c * GLA_CHUNK
        rows = slice(r0, r0 + GLA_CHUNK)
        g_last = g_cum[r0 + GLA_CHUNK - 1:r0 + GLA_CHUNK, :]
        k_h = (k[rows] * jnp.exp(g_last - g_cum[rows])).astype(BF16)
        dec = jnp.exp(g_last)
        for p in pairs:
            lanes = slice(p * LANES, (p + 1) * LANES)
            v_s2 = jnp.concatenate(
                [v_g[rows, (2 * p + hh) * GLA_DV:(2 * p + hh + 1) * GLA_DV] for hh in range(2)],
                axis=0)
            o_pair = _dot(att.pop((c, p)), v_s2) + _dot_nt(q_stack.pop((c, p)),
                                                           state[p].astype(BF16))
            o_rows[c][2 * p] = o_pair[:GLA_CHUNK]
            o_rows[c][2 * p + 1] = o_pair[GLA_CHUNK:]
            state[p] = dec[:, lanes] * state[p] + _dot_tn(v_s2, stack_heads(k_h[:, lanes]))

    st_ = lax.broadcasted_iota(jnp.int32, (SG_CHUNK, SG_CHUNK), 0)
    ss_ = lax.broadcasted_iota(jnp.int32, (SG_CHUNK, SG_CHUNK), 1)
    sg_causal = ss_ <= st_
    sg_half = SG_WIDTH // 2
    acts = {}

    def sg_proj(name, off, act, hf):
        acts[name, hf] = act(proj(off + hf * sg_half, sg_half))

    def sg_group(grp):
        hf, sl = divmod(grp * SG_GROUP_DIM, sg_half)
        sl = slice(sl, sl + SG_GROUP_DIM)
        gl = slice(grp * SG_GROUP_DIM, (grp + 1) * SG_GROUP_DIM)
        v_grp = acts["v", hf][:, sl]
        mu = jnp.mean(v_grp, axis=-1, keepdims=True)
        cen = v_grp - mu
        var = jnp.mean(cen * cen, axis=-1, keepdims=True)
        vn = (cen * lax.rsqrt(var + EPS) * lng[:, gl] + lnb[:, gl]).astype(BF16)
        w_g = jnp.where(sg_causal, ws_ref[grp], 0.0).astype(BF16)
        vn_wide = jnp.concatenate(
            [vn[n * SG_CHUNK:(n + 1) * SG_CHUNK, :] for n in range(n_sg_chunks)], axis=1)
        mixed_wide = _dot(w_g, vn_wide)
        bias = bsp_ref[:, gl]
        for n in range(n_sg_chunks):
            rs_ = slice(n * SG_CHUNK, (n + 1) * SG_CHUNK)
            mixed = mixed_wide[:, n * SG_GROUP_DIM:(n + 1) * SG_GROUP_DIM] + bias
            out = acts["u", hf][rs_, sl] * mixed * acts["r", hf][rs_, sl]
            yin_ref[rs_, GLA_WIDTH + grp * SG_GROUP_DIM:GLA_WIDTH + (grp + 1) * SG_GROUP_DIM] = (
                out.astype(BF16))

    def finish_rows(rh, n_split):
        rows = slice(rh * half_rows, (rh + 1) * half_rows)
        chunks = range(rh * n_gla_chunks // 2, (rh + 1) * n_gla_chunks // 2)
        for hd in range(GLA_HEADS):
            sl = slice(hd * GLA_DV, (hd + 1) * GLA_DV)
            o_h = jnp.concatenate([o_rows[c][hd] for c in chunks], axis=0)
            o_h = o_h * lax.rsqrt(jnp.mean(o_h * o_h, axis=-1, keepdims=True) + EPS) * gng[:, sl]
            yin_ref[rows, sl] = (o_h * r_g[rows, sl]).astype(BF16)
        width = D_MODEL // n_split
        parts = []
        for nh in range(n_split):
            cols = slice(nh * width, (nh + 1) * width)
            y = _dot(yin_ref[rows, :], wout_ref[:, cols])
            parts.append(xc_ref[0, rows, cols] + modc_ref[0, 2:3, cols] * y)
        if final:
            ms2 = sum(jnp.sum(part * part, axis=-1, keepdims=True) for part in parts) / D_MODEL
            scale_rows = lax.rsqrt(ms2 + EPS)
            parts = [part * scale_rows * fg_ref[:, nh * width:(nh + 1) * width]
                     for nh, part in enumerate(parts)]
        for nh, part in enumerate(parts):
            o_ref[0, rows, nh * width:(nh + 1) * width] = part

    fillers = [
        functools.partial(sg_proj, "v", _OFF_VS, _gelu, 0),
        functools.partial(sg_proj, "u", _OFF_US, _gelu, 0),
        functools.partial(sg_proj, "r", _OFF_RS, _silu, 0),
        functools.partial(sg_proj, "v", _OFF_VS, _gelu, 1),
        lambda: (sg_group(0), sg_group(1), sg_proj("u", _OFF_US, _gelu, 1)),
        functools.partial(sg_proj, "r", _OFF_RS, _silu, 1),
        lambda: (sg_group(2), sg_group(3)),
    ]
    first_half_at = max(len(fillers), n_gla_chunks // 2)
    assert first_half_at < n_gla_chunks
    gla_scores(0)
    for c in range(n_gla_chunks):
        if c + 1 < n_gla_chunks:
            gla_scores(c + 1)
        if fillers:
            fillers.pop(0)()
        if c == first_half_at:
            finish_rows(0, 1)
        gla_apply(c)
    for p in pairs:
        st_ref[p] = state[p]
    finish_rows(1, 2)


def _layer_call(x, mod, ng, win, wa2, ba, gng, lng, lnb, ws, bsp, wout, fg, *, final):
    b, s, d = x.shape
    ts = SEQ_TILE
    tiles_per_seq = s // ts
    n_tiles = b * tiles_per_seq
    xt = x.reshape(n_tiles, ts, d)
    const2 = lambda g: (0, 0)
    const3 = lambda g: (0, 0, 0)
    out = pl.pallas_call(
        functools.partial(_layer_kernel, final=final, tiles_per_seq=tiles_per_seq),
        grid=(n_tiles,),
        in_specs=[
            pl.BlockSpec((1, ts, d), lambda g: (g, 0, 0)),
            pl.BlockSpec((1, 3, d), lambda g: (g // tiles_per_seq, 0, 0)),
            pl.BlockSpec((1, d), const2),
            pl.BlockSpec((d, IN_WIDTH_PAD), const2),
            pl.BlockSpec((RANK_PAD, GLA_KW), const2),
            pl.BlockSpec((1, GLA_KW), const2),
            pl.BlockSpec((1, GLA_WIDTH), const2),
            pl.BlockSpec((1, SG_WIDTH), const2),
            pl.BlockSpec((1, SG_WIDTH), const2),
            pl.BlockSpec((SG_GROUPS, SG_CHUNK, SG_CHUNK), const3),
            pl.BlockSpec((SG_CHUNK, SG_WIDTH), const2),
            pl.BlockSpec((d, d), const2),
            pl.BlockSpec((1, d), const2),
        ],
        out_specs=pl.BlockSpec((1, ts, d), lambda g: (g, 0, 0)),
        out_shape=jax.ShapeDtypeStruct((n_tiles, ts, d), F32),
        scratch_shapes=[
            pltpu.VMEM((GLA_HEADS // 2, GLA_DV, LANES), F32),
            pltpu.VMEM((ts, d), BF16),
        ],
        compiler_params=pltpu.CompilerParams(
            dimension_semantics=("arbitrary",),
            vmem_limit_bytes=VMEM_LIMIT_BYTES),
        name="gla_gmlp_layer",
    )(xt, mod, ng, win, wa2, ba, gng, lng, lnb, ws, bsp, wout, fg)
    return out.reshape(b, s, d)


def kernel(x, c, norm_g, w_ada, b_ada, w_in, w_alpha2, b_alpha, gla_norm_g, sg_ln_g, sg_ln_b,
           w_spatial, b_spatial, w_out, final_g):
    depth = w_in.shape[0]
    bsz = x.shape[0]
    d = x.shape[-1]

    mod = _ada_call(c, w_ada, b_ada)
    mod = mod.reshape(depth, bsz, 3, d)

    o_q, o_k, o_vg = 0, GLA_KW, 2 * GLA_KW
    o_a = o_vg + GLA_WIDTH
    o_rest = o_a + GATE_RANK
    w_a_pad = jnp.pad(w_in[:, :, o_a:o_rest], ((0, 0), (0, 0), (0, RANK_PAD - GATE_RANK)))
    win = jnp.concatenate(
        [w_in[:, :, o_q:o_vg], w_a_pad, w_in[:, :, o_vg:o_a], w_in[:, :, o_rest:]],
        axis=-1).astype(BF16)
    wa2 = jnp.pad(w_alpha2, ((0, 0), (0, RANK_PAD - GATE_RANK), (0, 0))).astype(BF16)
    wout = w_out.astype(BF16)
    bsp = jnp.repeat(jnp.transpose(b_spatial, (0, 2, 1)), SG_GROUP_DIM, axis=-1)

    for l in range(depth):
        x = _layer_call(
            x, mod[l], norm_g[l][None], win[l], wa2[l], b_alpha[l][None], gla_norm_g[l][None],
            sg_ln_g[l][None], sg_ln_b[l][None], w_spatial[l], bsp[l], wout[l], final_g[None],
            final=(l == depth - 1))
    return x
```

```python
import functools

import jax
import jax.numpy as jnp
from jax import lax
from jax.experimental import pallas as pl
from jax.experimental.pallas import tpu as pltpu

F32 = jnp.float32
BF16 = jnp.bfloat16

D_MODEL = 1024
GLA_HEADS = 4
GLA_DK = 64
GLA_DV = 128
GLA_KW = GLA_HEADS * GLA_DK
GLA_WIDTH = GLA_HEADS * GLA_DV
GATE_RANK = 16
GATE_TAU = 16.0
GLA_CHUNK = 64
SG_GROUPS = 4
SG_GROUP_DIM = 128
SG_WIDTH = SG_GROUPS * SG_GROUP_DIM
SG_CHUNK = 128
EPS = 1e-6

LANES = 128
MXU_DIM = 256
RANK_PAD = LANES
SEQ_TILE = 1024
SUB_TILE = 512
VMEM_LIMIT_BYTES = 48 * 1024 * 1024

_OFF_Q = 0
_OFF_K = _OFF_Q + GLA_KW
_OFF_A = _OFF_K + GLA_KW
_OFF_VG = _OFF_A + RANK_PAD
_OFF_RG = _OFF_VG + GLA_WIDTH
_OFF_US = _OFF_RG + GLA_WIDTH
_OFF_VS = _OFF_US + SG_WIDTH
_OFF_RS = _OFF_VS + SG_WIDTH
IN_WIDTH_PAD = _OFF_RS + SG_WIDTH


def _split3(a):
    hi = a.astype(BF16)
    r1 = a - hi.astype(F32)
    mid = r1.astype(BF16)
    lo = (r1 - mid.astype(F32)).astype(BF16)
    return hi, mid, lo


def _split2(a):
    hi = a.astype(BF16)
    return hi, (a - hi.astype(F32)).astype(BF16)


def _dot(a, b):
    return jnp.dot(a, b, preferred_element_type=F32)


def _dot_nt(a, b):
    return lax.dot_general(a, b, (((1,), (1,)), ((), ())), preferred_element_type=F32)


def _dot_tn(a, b):
    return lax.dot_general(a, b, (((0,), (0,)), ((), ())), preferred_element_type=F32)


def _silu(a):
    return a * (1.0 / (1.0 + jnp.exp(-a)))


def _gelu(a):
    return 0.5 * a * (1.0 + lax.erf(a * (2.0 ** -0.5)))


def _ada_kernel(c_ref, w_ref, b_ref, o_ref):
    c = c_ref[...]
    ca = _silu(c)
    w = w_ref[0]
    a_hi, a_mid, a_lo = _split3(ca)
    w_hi, w_mid, w_lo = _split3(w)
    acc = _dot(a_hi, w_hi)
    acc += _dot(a_hi, w_mid) + _dot(a_mid, w_hi)
    acc += _dot(a_hi, w_lo) + _dot(a_mid, w_mid) + _dot(a_lo, w_hi)
    o_ref[0] = acc + b_ref[0]


def _ada_call(c, w_ada, b_ada):
    depth, d, d3 = w_ada.shape
    b = c.shape[0]
    nblk = d3 // d
    return pl.pallas_call(
        _ada_kernel,
        grid=(depth, nblk),
        in_specs=[
            pl.BlockSpec((b, d), lambda l, j: (0, 0)),
            pl.BlockSpec((1, d, d), lambda l, j: (l, 0, j)),
            pl.BlockSpec((1, 1, d), lambda l, j: (l, 0, j)),
        ],
        out_specs=pl.BlockSpec((1, b, d), lambda l, j: (l, 0, j)),
        out_shape=jax.ShapeDtypeStruct((depth, b, d3), F32),
        compiler_params=pltpu.CompilerParams(
            dimension_semantics=("arbitrary", "arbitrary"),
            vmem_limit_bytes=VMEM_LIMIT_BYTES),
        name="adaln_mod",
    )(c, w_ada, b_ada.reshape(depth, 1, d3))


PREP_ROWS = 256


def _prep_kernel(w_ref, o_ref):
    o_v = 2 * GLA_KW
    o_a = o_v + GLA_WIDTH
    o_r = o_a + GATE_RANK
    o_ref[0, :, _OFF_Q:_OFF_A] = w_ref[0, :, 0:o_v].astype(BF16)
    a_tile = w_ref[0, :, o_a:o_a + RANK_PAD]
    lane = lax.broadcasted_iota(jnp.int32, a_tile.shape, 1)
    o_ref[0, :, _OFF_A:_OFF_VG] = jnp.where(lane < GATE_RANK, a_tile, 0.0).astype(BF16)
    o_ref[0, :, _OFF_VG:_OFF_RG] = w_ref[0, :, o_v:o_a].astype(BF16)
    o_ref[0, :, _OFF_RG:] = w_ref[0, :, o_r:].astype(BF16)


def _prep_call(w_in):
    depth, d, n_in = w_in.shape
    return pl.pallas_call(
        _prep_kernel,
        grid=(depth, d // PREP_ROWS),
        in_specs=[pl.BlockSpec((1, PREP_ROWS, n_in), lambda l, i: (l, i, 0))],
        out_specs=pl.BlockSpec((1, PREP_ROWS, IN_WIDTH_PAD), lambda l, i: (l, i, 0)),
        out_shape=jax.ShapeDtypeStruct((depth, d, IN_WIDTH_PAD), BF16),
        compiler_params=pltpu.CompilerParams(
            dimension_semantics=("arbitrary", "arbitrary"),
            vmem_limit_bytes=VMEM_LIMIT_BYTES),
        name="w_in_relayout",
    )(w_in)


def _layer_kernel(xc_ref, modc_ref, ng_ref, win_ref, wa2_ref, ba_ref, gng_ref,
                  lng_ref, lnb_ref, ws_ref, bsp_ref, wout_ref, fg_ref, o_ref,
                  st_ref, yin_ref, hb_ref, *, final, tiles_per_seq):
    g = pl.program_id(0)
    ts = xc_ref.shape[1]
    sub = min(SUB_TILE, ts)
    n_gla_chunks = sub // GLA_CHUNK
    n_sg_chunks = sub // SG_CHUNK
    half_rows = sub // 2
    pairs = range(GLA_HEADS // 2)

    @pl.when(g % tiles_per_seq == 0)
    def _():
        st_ref[...] = jnp.zeros_like(st_ref)

    gng = gng_ref[...]
    lng = lng_ref[...]
    lnb = lnb_ref[...]
    shift = modc_ref[0, 0:1, :]
    scale = modc_ref[0, 1:2, :]
    norm_gain = ng_ref[...] * (1.0 + scale)

    row = lax.broadcasted_iota(jnp.int32, (MXU_DIM, MXU_DIM), 0)
    col = lax.broadcasted_iota(jnp.int32, (MXU_DIM, MXU_DIM), 1)
    same_chunk = (row // GLA_CHUNK) == (col // GLA_CHUNK)
    cum_mat = jnp.where(same_chunk & (col <= row), 1.0, 0.0).astype(BF16)

    pr = lax.broadcasted_iota(jnp.int32, (2 * GLA_CHUNK, LANES), 0)
    pc = lax.broadcasted_iota(jnp.int32, (2 * GLA_CHUNK, LANES), 1)
    own_lanes = (pr // GLA_CHUNK) == (pc // GLA_DK)
    pair_causal = own_lanes & ((pc % GLA_CHUNK) <= (pr % GLA_CHUNK))

    def stack_heads(a):
        a2 = jnp.concatenate([a, a], axis=0)
        return jnp.where(own_lanes, a2, jnp.zeros_like(a2))

    st_ = lax.broadcasted_iota(jnp.int32, (SG_CHUNK, SG_CHUNK), 0)
    ss_ = lax.broadcasted_iota(jnp.int32, (SG_CHUNK, SG_CHUNK), 1)
    sg_causal = ss_ <= st_
    sg_half = SG_WIDTH // 2

    state = [st_ref[p] for p in pairs]

    def sub_tile(base):
        xx = xc_ref[0, base:base + sub, :]
        ms = jnp.mean(xx * xx, axis=-1, keepdims=True)
        hb_ref[base:base + sub, :] = (
            (xx * lax.rsqrt(ms + EPS)) * norm_gain + shift).astype(BF16)
        yield

        def proj(off, width, rows=slice(None)):
            start, stop, _ = rows.indices(sub)
            return _dot(hb_ref[base + start:base + stop, :], win_ref[:, off:off + width])

        a_lr = jnp.concatenate(
            [proj(_OFF_A, RANK_PAD, slice(0, half_rows)),
             proj(_OFF_A, RANK_PAD, slice(half_rows, sub))], axis=0)
        z = _dot(a_lr.astype(BF16), wa2_ref[...]) + ba_ref[...]
        qk = proj(_OFF_Q, 2 * GLA_KW)
        q = qk[:, :GLA_KW]
        k = qk[:, GLA_KW:]
        log_a = (jnp.minimum(z, 0.0) - jnp.log(1.0 + jnp.exp(-jnp.abs(z)))) * (1.0 / GATE_TAU)
        v_g = proj(_OFF_VG, GLA_WIDTH).astype(BF16)

        la_parts = _split2(log_a)
        g_cum = jnp.concatenate(
            [sum(_dot(cum_mat, part[r:r + MXU_DIM]) for part in la_parts)
             for r in range(0, sub, MXU_DIM)], axis=0)
        r_g = _silu(proj(_OFF_RG, GLA_WIDTH))

        q_t = (q * jnp.exp(g_cum) * (GLA_DK ** -0.5)).astype(BF16)
        k_t = (k * jnp.exp(-g_cum)).astype(BF16)

        q_stack = {}
        att = {}
        o_rows = [[None] * GLA_HEADS for _ in range(n_gla_chunks)]

        def gla_scores(c):
            rows = slice(c * GLA_CHUNK, (c + 1) * GLA_CHUNK)
            for p in pairs:
                lanes = slice(p * LANES, (p + 1) * LANES)
                q_stack[c, p] = stack_heads(q_t[rows, lanes])
                s = _dot_nt(q_stack[c, p], stack_heads(k_t[rows, lanes]))
                att[c, p] = jnp.where(pair_causal, s, 0.0).astype(BF16)

        def gla_apply(c):
            r0 = c * GLA_CHUNK
            rows = slice(r0, r0 + GLA_CHUNK)
            g_last = g_cum[r0 + GLA_CHUNK - 1:r0 + GLA_CHUNK, :]
            k_h = (k[rows] * jnp.exp(g_last - g_cum[rows])).astype(BF16)
            dec = jnp.exp(g_last)
            for p in pairs:
                lanes = slice(p * LANES, (p + 1) * LANES)
                v_s2 = jnp.concatenate(
                    [v_g[rows, (2 * p + hh) * GLA_DV:(2 * p + hh + 1) * GLA_DV]
                     for hh in range(2)], axis=0)
                o_pair = _dot(att.pop((c, p)), v_s2) + _dot_nt(q_stack.pop((c, p)),
                                                               state[p].astype(BF16))
                o_rows[c][2 * p] = o_pair[:GLA_CHUNK]
                o_rows[c][2 * p + 1] = o_pair[GLA_CHUNK:]
                state[p] = dec[:, lanes] * state[p] + _dot_tn(v_s2, stack_heads(k_h[:, lanes]))

        acts = {}

        def sg_proj(name, off, act, hf):
            acts[name, hf] = act(proj(off + hf * sg_half, sg_half))

        def sg_group(grp):
            hf, sl = divmod(grp * SG_GROUP_DIM, sg_half)
            sl = slice(sl, sl + SG_GROUP_DIM)
            gl = slice(grp * SG_GROUP_DIM, (grp + 1) * SG_GROUP_DIM)
            v_grp = acts["v", hf][:, sl]
            mu = jnp.mean(v_grp, axis=-1, keepdims=True)
            cen = v_grp - mu
            var = jnp.mean(cen * cen, axis=-1, keepdims=True)
            vn = (cen * lax.rsqrt(var + EPS) * lng[:, gl] + lnb[:, gl]).astype(BF16)
            w_g = jnp.where(sg_causal, ws_ref[grp], 0.0).astype(BF16)
            vn_wide = jnp.concatenate(
                [vn[n * SG_CHUNK:(n + 1) * SG_CHUNK, :] for n in range(n_sg_chunks)], axis=1)
            mixed_wide = _dot(w_g, vn_wide)
            bias = bsp_ref[:, gl]
            for n in range(n_sg_chunks):
                rs_ = slice(n * SG_CHUNK, (n + 1) * SG_CHUNK)
                mixed = mixed_wide[:, n * SG_GROUP_DIM:(n + 1) * SG_GROUP_DIM] + bias
                out = acts["u", hf][rs_, sl] * mixed * acts["r", hf][rs_, sl]
                yin_ref[base + n * SG_CHUNK:base + (n + 1) * SG_CHUNK,
                        GLA_WIDTH + grp * SG_GROUP_DIM:GLA_WIDTH + (grp + 1) * SG_GROUP_DIM] = (
                    out.astype(BF16))

        def finish_rows(rh, n_split):
            rows = slice(rh * half_rows, (rh + 1) * half_rows)
            out_rows = slice(base + rh * half_rows, base + (rh + 1) * half_rows)
            chunks = range(rh * n_gla_chunks // 2, (rh + 1) * n_gla_chunks // 2)
            for hd in range(GLA_HEADS):
                sl = slice(hd * GLA_DV, (hd + 1) * GLA_DV)
                o_h = jnp.concatenate([o_rows[c][hd] for c in chunks], axis=0)
                o_h = (o_h * lax.rsqrt(jnp.mean(o_h * o_h, axis=-1, keepdims=True) + EPS)
                       * gng[:, sl])
                yin_ref[out_rows, sl] = (o_h * r_g[rows, sl]).astype(BF16)
            width = D_MODEL // n_split
            parts = []
            for nh in range(n_split):
                cols = slice(nh * width, (nh + 1) * width)
                y = _dot(yin_ref[out_rows, :], wout_ref[:, cols])
                parts.append(xc_ref[0, out_rows, cols] + modc_ref[0, 2:3, cols] * y)
            if final:
                ms2 = sum(jnp.sum(part * part, axis=-1, keepdims=True) for part in parts) / D_MODEL
                scale_rows = lax.rsqrt(ms2 + EPS)
                parts = [part * scale_rows * fg_ref[:, nh * width:(nh + 1) * width]
                         for nh, part in enumerate(parts)]
            for nh, part in enumerate(parts):
                o_ref[0, out_rows, nh * width:(nh + 1) * width] = part

        fillers = [
            functools.partial(sg_proj, "v", _OFF_VS, _gelu, 0),
            functools.partial(sg_proj, "u", _OFF_US, _gelu, 0),
            functools.partial(sg_proj, "r", _OFF_RS, _silu, 0),
            functools.partial(sg_proj, "v", _OFF_VS, _gelu, 1),
            lambda: (sg_group(0), sg_group(1), sg_proj("u", _OFF_US, _gelu, 1)),
            functools.partial(sg_proj, "r", _OFF_RS, _silu, 1),
            lambda: (sg_group(2), sg_group(3)),
        ]
        first_half_at = max(len(fillers), n_gla_chunks // 2)
        assert first_half_at < n_gla_chunks
        gla_scores(0)
        for c in range(n_gla_chunks):
            if c + 1 < n_gla_chunks:
                gla_scores(c + 1)
            if fillers:
                fillers.pop(0)()
            if c == first_half_at:
                finish_rows(0, 1)
            gla_apply(c)
        yield
        finish_rows(1, 2)

    tiles = [sub_tile(base) for base in range(0, ts, sub)]
    next(tiles[0])
    for i, tile in enumerate(tiles):
        next(tile)
        if i + 1 < len(tiles):
            next(tiles[i + 1])
        for _ in tile:
            pass
    for p in pairs:
        st_ref[p] = state[p]


def _layer_call(x, mod, ng, win, wa2, ba, gng, lng, lnb, ws, bsp, wout, fg, *, final):
    b, s, d = x.shape
    ts = SEQ_TILE
    tiles_per_seq = s // ts
    n_tiles = b * tiles_per_seq
    xt = x.reshape(n_tiles, ts, d)
    const2 = lambda g: (0, 0)
    const3 = lambda g: (0, 0, 0)
    out = pl.pallas_call(
        functools.partial(_layer_kernel, final=final, tiles_per_seq=tiles_per_seq),
        grid=(n_tiles,),
        in_specs=[
            pl.BlockSpec((1, ts, d), lambda g: (g, 0, 0)),
            pl.BlockSpec((1, 3, d), lambda g: (g // tiles_per_seq, 0, 0)),
            pl.BlockSpec((1, d), const2),
            pl.BlockSpec((d, IN_WIDTH_PAD), const2),
            pl.BlockSpec((RANK_PAD, GLA_KW), const2),
            pl.BlockSpec((1, GLA_KW), const2),
            pl.BlockSpec((1, GLA_WIDTH), const2),
            pl.BlockSpec((1, SG_WIDTH), const2),
            pl.BlockSpec((1, SG_WIDTH), const2),
            pl.BlockSpec((SG_GROUPS, SG_CHUNK, SG_CHUNK), const3),
            pl.BlockSpec((SG_CHUNK, SG_WIDTH), const2),
            pl.BlockSpec((d, d), const2),
            pl.BlockSpec((1, d), const2),
        ],
        out_specs=pl.BlockSpec((1, ts, d), lambda g: (g, 0, 0)),
        out_shape=jax.ShapeDtypeStruct((n_tiles, ts, d), F32),
        scratch_shapes=[
            pltpu.VMEM((GLA_HEADS // 2, GLA_DV, LANES), F32),
            pltpu.VMEM((ts, d), BF16),
            pltpu.VMEM((ts, d), BF16),
        ],
        compiler_params=pltpu.CompilerParams(
            dimension_semantics=("arbitrary",),
            vmem_limit_bytes=VMEM_LIMIT_BYTES),
        name="gla_gmlp_layer",
    )(xt, mod, ng, win, wa2, ba, gng, lng, lnb, ws, bsp, wout, fg)
    return out.reshape(b, s, d)


def kernel(x, c, norm_g, w_ada, b_ada, w_in, w_alpha2, b_alpha, gla_norm_g, sg_ln_g, sg_ln_b,
           w_spatial, b_spatial, w_out, final_g):
    depth = w_in.shape[0]
    bsz = x.shape[0]
    d = x.shape[-1]

    mod = _ada_call(c, w_ada, b_ada)
    mod = mod.reshape(depth, bsz, 3, d)

    win = _prep_call(w_in)
    wa2 = jnp.pad(w_alpha2, ((0, 0), (0, RANK_PAD - GATE_RANK), (0, 0))).astype(BF16)
    wout = w_out.astype(BF16)
    bsp = jnp.repeat(jnp.transpose(b_spatial, (0, 2, 1)), SG_GROUP_DIM, axis=-1)

    for l in range(depth):
        x = _layer_call(
            x, mod[l], norm_g[l][None], win[l], wa2[l], b_alpha[l][None], gla_norm_g[l][None],
            sg_ln_g[l][None], sg_ln_b[l][None], w_spatial[l], bsp[l], wout[l], final_g[None],
            final=(l == depth - 1))
    return x
```

```python
import functools

import jax
import jax.numpy as jnp
from jax import lax
from jax.experimental import pallas as pl
from jax.experimental.pallas import tpu as pltpu

F32 = jnp.float32
BF16 = jnp.bfloat16

D_MODEL = 1024
GLA_HEADS = 4
GLA_DK = 64
GLA_DV = 128
GLA_KW = GLA_HEADS * GLA_DK
GLA_WIDTH = GLA_HEADS * GLA_DV
GATE_RANK = 16
GATE_TAU = 16.0
GLA_CHUNK = 64
SG_GROUPS = 4
SG_GROUP_DIM = 128
SG_WIDTH = SG_GROUPS * SG_GROUP_DIM
SG_CHUNK = 128
EPS = 1e-6

LANES = 128
MXU_DIM = 256
RANK_PAD = LANES
SEQ_TILE = 1024
SUB_TILE = 512
VMEM_LIMIT_BYTES = 48 * 1024 * 1024

_OFF_Q = 0
_OFF_K = _OFF_Q + GLA_KW
_OFF_A = _OFF_K + GLA_KW
_OFF_VG = _OFF_A + RANK_PAD
_OFF_RG = _OFF_VG + GLA_WIDTH
_OFF_US = _OFF_RG + GLA_WIDTH
_OFF_VS = _OFF_US + SG_WIDTH
_OFF_RS = _OFF_VS + SG_WIDTH
IN_WIDTH_PAD = _OFF_RS + SG_WIDTH


def _split3(a):
    hi = a.astype(BF16)
    r1 = a - hi.astype(F32)
    mid = r1.astype(BF16)
    lo = (r1 - mid.astype(F32)).astype(BF16)
    return hi, mid, lo


def _split2(a):
    hi = a.astype(BF16)
    return hi, (a - hi.astype(F32)).astype(BF16)


def _dot(a, b):
    return jnp.dot(a, b, preferred_element_type=F32)


def _dot_nt(a, b):
    return lax.dot_general(a, b, (((1,), (1,)), ((), ())), preferred_element_type=F32)


def _dot_tn(a, b):
    return lax.dot_general(a, b, (((0,), (0,)), ((), ())), preferred_element_type=F32)


def _silu(a):
    return a * (1.0 / (1.0 + jnp.exp(-a)))


def _gelu(a):
    return 0.5 * a * (1.0 + lax.erf(a * (2.0 ** -0.5)))


def _ada_kernel(c_ref, w_ref, b_ref, o_ref):
    c = c_ref[...]
    ca = _silu(c)
    w = w_ref[0]
    a_hi, a_mid, a_lo = _split3(ca)
    w_hi, w_mid, w_lo = _split3(w)
    acc = _dot(a_hi, w_hi)
    acc += _dot(a_hi, w_mid) + _dot(a_mid, w_hi)
    acc += _dot(a_hi, w_lo) + _dot(a_mid, w_mid) + _dot(a_lo, w_hi)
    o_ref[0] = acc + b_ref[0]


def _ada_call(c, w_ada, b_ada):
    depth, d, d3 = w_ada.shape
    b = c.shape[0]
    nblk = d3 // d
    return pl.pallas_call(
        _ada_kernel,
        grid=(depth, nblk),
        in_specs=[
            pl.BlockSpec((b, d), lambda l, j: (0, 0)),
            pl.BlockSpec((1, d, d), lambda l, j: (l, 0, j)),
            pl.BlockSpec((1, 1, d), lambda l, j: (l, 0, j)),
        ],
        out_specs=pl.BlockSpec((1, b, d), lambda l, j: (l, 0, j)),
        out_shape=jax.ShapeDtypeStruct((depth, b, d3), F32),
        compiler_params=pltpu.CompilerParams(
            dimension_semantics=("arbitrary", "arbitrary"),
            vmem_limit_bytes=VMEM_LIMIT_BYTES),
        name="adaln_mod",
    )(c, w_ada, b_ada.reshape(depth, 1, d3))


PREP_K = 256


def _prep_kernel(wt_ref, o_ref):
    o_v = 2 * GLA_KW
    o_a = o_v + GLA_WIDTH
    o_r = o_a + GATE_RANK
    piece = 2 * GLA_KW

    def put(dst, src, width=piece):
        o_ref[0, :, dst:dst + width] = wt_ref[0, src:src + width, :].T.astype(BF16)

    put(_OFF_Q, 0)
    a_tile = wt_ref[0, o_a:o_a + RANK_PAD, :].T
    lane = lax.broadcasted_iota(jnp.int32, a_tile.shape, 1)
    o_ref[0, :, _OFF_A:_OFF_VG] = jnp.where(lane < GATE_RANK, a_tile, 0.0).astype(BF16)
    put(_OFF_VG, o_v)
    for i in range((IN_WIDTH_PAD - _OFF_RG) // piece):
        put(_OFF_RG + i * piece, o_r + i * piece)


def _prep_call(w_in):
    depth, d, n_in = w_in.shape
    wt = jnp.swapaxes(w_in, 1, 2)
    return pl.pallas_call(
        _prep_kernel,
        grid=(depth, d // PREP_K),
        in_specs=[pl.BlockSpec((1, n_in, PREP_K), lambda l, i: (l, 0, i))],
        out_specs=pl.BlockSpec((1, PREP_K, IN_WIDTH_PAD), lambda l, i: (l, i, 0)),
        out_shape=jax.ShapeDtypeStruct((depth, d, IN_WIDTH_PAD), BF16),
        compiler_params=pltpu.CompilerParams(
            dimension_semantics=("arbitrary", "arbitrary"),
            vmem_limit_bytes=VMEM_LIMIT_BYTES),
        name="w_in_relayout",
    )(wt)


def _layer_kernel(xc_ref, modc_ref, ng_ref, win_ref, wa2_ref, ba_ref, gng_ref,
                  lng_ref, lnb_ref, ws_ref, bsp_ref, wout_ref, fg_ref, o_ref,
                  st_ref, yin_ref, hb_ref, *, final, tiles_per_seq):
    g = pl.program_id(0)
    ts = xc_ref.shape[1]
    sub = min(SUB_TILE, ts)
    n_gla_chunks = sub // GLA_CHUNK
    n_sg_chunks = sub // SG_CHUNK
    half_rows = sub // 2
    pairs = range(GLA_HEADS // 2)

    @pl.when(g % tiles_per_seq == 0)
    def _():
        st_ref[...] = jnp.zeros_like(st_ref)

    gng = gng_ref[0]
    lng = lng_ref[0]
    lnb = lnb_ref[0]
    shift = modc_ref[0, 0, 0:1, :]
    scale = modc_ref[0, 0, 1:2, :]
    norm_gain = ng_ref[0] * (1.0 + scale)

    row = lax.broadcasted_iota(jnp.int32, (MXU_DIM, MXU_DIM), 0)
    col = lax.broadcasted_iota(jnp.int32, (MXU_DIM, MXU_DIM), 1)
    same_chunk = (row // GLA_CHUNK) == (col // GLA_CHUNK)
    cum_mat = jnp.where(same_chunk & (col <= row), 1.0, 0.0).astype(BF16)

    pr = lax.broadcasted_iota(jnp.int32, (2 * GLA_CHUNK, LANES), 0)
    pc = lax.broadcasted_iota(jnp.int32, (2 * GLA_CHUNK, LANES), 1)
    own_lanes = (pr // GLA_CHUNK) == (pc // GLA_DK)
    at_ = lax.broadcasted_iota(jnp.int32, (GLA_CHUNK, 2 * GLA_CHUNK), 0)
    as_ = lax.broadcasted_iota(jnp.int32, (GLA_CHUNK, 2 * GLA_CHUNK), 1)
    pair_causal = (as_ % GLA_CHUNK) <= at_

    def stack_heads(a):
        a2 = jnp.concatenate([a, a], axis=0)
        return jnp.where(own_lanes, a2, jnp.zeros_like(a2))

    st_ = lax.broadcasted_iota(jnp.int32, (SG_CHUNK, SG_CHUNK), 0)
    ss_ = lax.broadcasted_iota(jnp.int32, (SG_CHUNK, SG_CHUNK), 1)
    sg_causal = ss_ <= st_
    sg_half = SG_WIDTH // 2

    state = [st_ref[p] for p in pairs]

    def sub_tile(base):
        xx = xc_ref[0, base:base + sub, :]
        ms = jnp.mean(xx * xx, axis=-1, keepdims=True)
        hb_ref[base:base + sub, :] = (
            (xx * lax.rsqrt(ms + EPS)) * norm_gain + shift).astype(BF16)
        yield

        def proj(off, width, rows=slice(None)):
            start, stop, _ = rows.indices(sub)
            return _dot(hb_ref[base + start:base + stop, :], win_ref[0, :, off:off + width])

        a_lr = jnp.concatenate(
            [proj(_OFF_A, RANK_PAD, slice(0, half_rows)),
             proj(_OFF_A, RANK_PAD, slice(half_rows, sub))], axis=0)
        z = _dot(a_lr.astype(BF16), wa2_ref[0]) + ba_ref[0]
        qk = proj(_OFF_Q, 2 * GLA_KW)
        q = qk[:, :GLA_KW]
        k = qk[:, GLA_KW:]
        log_a = (jnp.minimum(z, 0.0) - jnp.log(1.0 + jnp.exp(-jnp.abs(z)))) * (1.0 / GATE_TAU)
        v_g = proj(_OFF_VG, GLA_WIDTH).astype(BF16)

        la_parts = _split2(log_a)
        g_cum = jnp.concatenate(
            [sum(_dot(cum_mat, part[r:r + MXU_DIM]) for part in la_parts)
             for r in range(0, sub, MXU_DIM)], axis=0)
        r_g = _silu(proj(_OFF_RG, GLA_WIDTH))

        q_t = (q * jnp.exp(g_cum) * (GLA_DK ** -0.5)).astype(BF16)
        k_t = (k * jnp.exp(-g_cum)).astype(BF16)

        last_rows = jnp.concatenate(
            [g_cum[(c + 1) * GLA_CHUNK - 1:(c + 1) * GLA_CHUNK, :] for c in range(n_gla_chunks)],
            axis=0)
        dec_cols = jnp.exp(last_rows).T

        att = {}
        o_rows = [[None] * len(pairs) for _ in range(n_gla_chunks)]

        def gla_scores(c):
            rows = slice(c * GLA_CHUNK, (c + 1) * GLA_CHUNK)
            for p in pairs:
                lanes = slice(p * LANES, (p + 1) * LANES)
                s = _dot_nt(q_t[rows, lanes], stack_heads(k_t[rows, lanes]))
                att[c, p] = jnp.where(pair_causal, s, 0.0).astype(BF16)

        def gla_apply(c):
            rows = slice(c * GLA_CHUNK, (c + 1) * GLA_CHUNK)
            k_h = (k[rows] * jnp.exp(last_rows[c:c + 1, :] - g_cum[rows])).astype(BF16)
            for p in pairs:
                lanes = slice(p * LANES, (p + 1) * LANES)
                v0 = v_g[rows, (2 * p) * GLA_DV:(2 * p + 1) * GLA_DV]
                v1 = v_g[rows, (2 * p + 1) * GLA_DV:(2 * p + 2) * GLA_DV]
                zero = jnp.zeros_like(v0)
                s_b = state[p].astype(BF16)
                rhs = jnp.concatenate(
                    [jnp.concatenate([v0, zero], axis=1),
                     jnp.concatenate([zero, v1], axis=1),
                     jnp.concatenate([s_b[:GLA_DK], zero], axis=1),
                     jnp.concatenate([zero, s_b[GLA_DK:]], axis=1)], axis=0)
                lhs = jnp.concatenate([att.pop((c, p)), q_t[rows, lanes]], axis=1)
                o_rows[c][p] = _dot(lhs, rhs)
                state[p] = (dec_cols[lanes, c:c + 1] * state[p]
                            + _dot_tn(stack_heads(k_h[:, lanes]),
                                      jnp.concatenate([v0, v1], axis=0)))

        acts = {}

        def sg_proj(name, off, act, hf):
            acts[name, hf] = act(proj(off + hf * sg_half, sg_half))

        def sg_group(grp):
            hf, sl = divmod(grp * SG_GROUP_DIM, sg_half)
            sl = slice(sl, sl + SG_GROUP_DIM)
            gl = slice(grp * SG_GROUP_DIM, (grp + 1) * SG_GROUP_DIM)
            v_grp = acts["v", hf][:, sl]
            mu = jnp.mean(v_grp, axis=-1, keepdims=True)
            cen = v_grp - mu
            var = jnp.mean(cen * cen, axis=-1, keepdims=True)
            vn = (cen * lax.rsqrt(var + EPS) * lng[:, gl] + lnb[:, gl]).astype(BF16)
            w_g = jnp.where(sg_causal, ws_ref[0, grp], 0.0).astype(BF16)
            vn_wide = jnp.concatenate(
                [vn[n * SG_CHUNK:(n + 1) * SG_CHUNK, :] for n in range(n_sg_chunks)], axis=1)
            mixed_wide = _dot(w_g, vn_wide)
            bias = bsp_ref[0, :, gl]
            for n in range(n_sg_chunks):
                rs_ = slice(n * SG_CHUNK, (n + 1) * SG_CHUNK)
                mixed = mixed_wide[:, n * SG_GROUP_DIM:(n + 1) * SG_GROUP_DIM] + bias
                out = acts["u", hf][rs_, sl] * mixed * acts["r", hf][rs_, sl]
                yin_ref[base + n * SG_CHUNK:base + (n + 1) * SG_CHUNK,
                        GLA_WIDTH + grp * SG_GROUP_DIM:GLA_WIDTH + (grp + 1) * SG_GROUP_DIM] = (
                    out.astype(BF16))

        def finish_rows(rh, n_split):
            rows = slice(rh * half_rows, (rh + 1) * half_rows)
            out_rows = slice(base + rh * half_rows, base + (rh + 1) * half_rows)
            chunks = range(rh * n_gla_chunks // 2, (rh + 1) * n_gla_chunks // 2)
            for hd in range(GLA_HEADS):
                sl = slice(hd * GLA_DV, (hd + 1) * GLA_DV)
                o_h = jnp.concatenate(
                    [o_rows[c][hd // 2][:, (hd % 2) * GLA_DV:(hd % 2 + 1) * GLA_DV]
                     for c in chunks], axis=0)
                o_h = (o_h * lax.rsqrt(jnp.mean(o_h * o_h, axis=-1, keepdims=True) + EPS)
                       * gng[:, sl])
                yin_ref[out_rows, sl] = (o_h * r_g[rows, sl]).astype(BF16)
            width = D_MODEL // n_split
            parts = []
            for nh in range(n_split):
                cols = slice(nh * width, (nh + 1) * width)
                y = _dot(yin_ref[out_rows, :], wout_ref[0, :, cols])
                parts.append(xc_ref[0, out_rows, cols] + modc_ref[0, 0, 2:3, cols] * y)
            if final:
                ms2 = sum(jnp.sum(part * part, axis=-1, keepdims=True) for part in parts) / D_MODEL
                scale_rows = lax.rsqrt(ms2 + EPS)
                parts = [part * scale_rows * fg_ref[:, nh * width:(nh + 1) * width]
                         for nh, part in enumerate(parts)]
            for nh, part in enumerate(parts):
                o_ref[0, out_rows, nh * width:(nh + 1) * width] = part

        fillers = [
            functools.partial(sg_proj, "v", _OFF_VS, _gelu, 0),
            functools.partial(sg_proj, "u", _OFF_US, _gelu, 0),
            functools.partial(sg_proj, "r", _OFF_RS, _silu, 0),
            functools.partial(sg_proj, "v", _OFF_VS, _gelu, 1),
            lambda: (sg_group(0), sg_group(1), sg_proj("u", _OFF_US, _gelu, 1)),
            functools.partial(sg_proj, "r", _OFF_RS, _silu, 1),
            lambda: (sg_group(2), sg_group(3)),
        ]
        first_half_at = max(len(fillers), n_gla_chunks // 2)
        assert first_half_at < n_gla_chunks
        gla_scores(0)
        for c in range(n_gla_chunks):
            if c + 1 < n_gla_chunks:
                gla_scores(c + 1)
            if fillers:
                fillers.pop(0)()
            if c == first_half_at:
                finish_rows(0, 1)
            gla_apply(c)
        yield
        finish_rows(1, 2)

    tiles = [sub_tile(base) for base in range(0, ts, sub)]
    next(tiles[0])
    for i, tile in enumerate(tiles):
        next(tile)
        if i + 1 < len(tiles):
            next(tiles[i + 1])
        for _ in tile:
            pass
    for p in pairs:
        st_ref[p] = state[p]


def _layer_call(layer, x, mod, ng, win, wa2, ba, gng, lng, lnb, ws, bsp, wout, fg, *, final):
    b, s, d = x.shape
    ts = SEQ_TILE
    tiles_per_seq = s // ts
    n_tiles = b * tiles_per_seq
    xt = x.reshape(n_tiles, ts, d)
    lay3 = lambda g: (layer, 0, 0)
    lay4 = lambda g: (layer, 0, 0, 0)
    out = pl.pallas_call(
        functools.partial(_layer_kernel, final=final, tiles_per_seq=tiles_per_seq),
        grid=(n_tiles,),
        in_specs=[
            pl.BlockSpec((1, ts, d), lambda g: (g, 0, 0)),
            pl.BlockSpec((1, 1, 3, d), lambda g: (layer, g // tiles_per_seq, 0, 0)),
            pl.BlockSpec((1, 1, d), lay3),
            pl.BlockSpec((1, d, IN_WIDTH_PAD), lay3),
            pl.BlockSpec((1, RANK_PAD, GLA_KW), lay3),
            pl.BlockSpec((1, 1, GLA_KW), lay3),
            pl.BlockSpec((1, 1, GLA_WIDTH), lay3),
            pl.BlockSpec((1, 1, SG_WIDTH), lay3),
            pl.BlockSpec((1, 1, SG_WIDTH), lay3),
            pl.BlockSpec((1, SG_GROUPS, SG_CHUNK, SG_CHUNK), lay4),
            pl.BlockSpec((1, SG_CHUNK, SG_WIDTH), lay3),
            pl.BlockSpec((1, d, d), lay3),
            pl.BlockSpec((1, d), lambda g: (0, 0)),
        ],
        out_specs=pl.BlockSpec((1, ts, d), lambda g: (g, 0, 0)),
        out_shape=jax.ShapeDtypeStruct((n_tiles, ts, d), F32),
        scratch_shapes=[
            pltpu.VMEM((GLA_HEADS // 2, 2 * GLA_DK, GLA_DV), F32),
            pltpu.VMEM((ts, d), BF16),
            pltpu.VMEM((ts, d), BF16),
        ],
        compiler_params=pltpu.CompilerParams(
            dimension_semantics=("arbitrary",),
            vmem_limit_bytes=VMEM_LIMIT_BYTES),
        name="gla_gmlp_layer",
    )(xt, mod, ng, win, wa2, ba, gng, lng, lnb, ws, bsp, wout, fg)
    return out.reshape(b, s, d)


def kernel(x, c, norm_g, w_ada, b_ada, w_in, w_alpha2, b_alpha, gla_norm_g, sg_ln_g, sg_ln_b,
           w_spatial, b_spatial, w_out, final_g):
    depth = w_in.shape[0]
    bsz = x.shape[0]
    d = x.shape[-1]

    mod = _ada_call(c, w_ada, b_ada).reshape(depth, bsz, 3, d)
    win = _prep_call(w_in)
    wa2 = jnp.pad(w_alpha2, ((0, 0), (0, RANK_PAD - GATE_RANK), (0, 0))).astype(BF16)
    wout = w_out.astype(BF16)
    bsp = jnp.repeat(jnp.transpose(b_spatial, (0, 2, 1)), SG_GROUP_DIM, axis=-1)
    row3 = lambda a: a[:, None, :]

    for l in range(depth):
        x = _layer_call(
            l, x, mod, row3(norm_g), win, wa2, row3(b_alpha), row3(gla_norm_g), row3(sg_ln_g),
            row3(sg_ln_b), w_spatial, bsp, wout, final_g[None], final=(l == depth - 1))
    return x
```

```python
import functools

import jax
import jax.numpy as jnp
from jax import lax
from jax.experimental import pallas as pl
from jax.experimental.pallas import tpu as pltpu

F32 = jnp.float32
BF16 = jnp.bfloat16

D_MODEL = 1024
GLA_HEADS = 4
GLA_DK = 64
GLA_DV = 128
GLA_KW = GLA_HEADS * GLA_DK
GLA_WIDTH = GLA_HEADS * GLA_DV
GATE_RANK = 16
GATE_TAU = 16.0
GLA_CHUNK = 64
SG_GROUPS = 4
SG_GROUP_DIM = 128
SG_WIDTH = SG_GROUPS * SG_GROUP_DIM
SG_CHUNK = 128
EPS = 1e-6

LANES = 128
MXU_DIM = 256
SEQ_TILE = 1024
SUB_TILE = 512
VMEM_LIMIT_BYTES = 48 * 1024 * 1024

_OFF_Q = 0
_OFF_K = _OFF_Q + GLA_KW
_OFF_VG = _OFF_K + GLA_KW
_OFF_RG = _OFF_VG + GLA_WIDTH
_OFF_US = _OFF_RG + GLA_WIDTH
_OFF_VS = _OFF_US + SG_WIDTH
_OFF_RS = _OFF_VS + SG_WIDTH
IN_WIDTH_MAIN = _OFF_RS + SG_WIDTH


def _split2(a):
    hi = a.astype(BF16)
    return hi, (a - hi.astype(F32)).astype(BF16)


def _dot(a, b):
    return jnp.dot(a, b, preferred_element_type=F32)


def _dot_nt(a, b):
    return lax.dot_general(a, b, (((1,), (1,)), ((), ())), preferred_element_type=F32)


def _dot_tn(a, b):
    return lax.dot_general(a, b, (((0,), (0,)), ((), ())), preferred_element_type=F32)


def _silu(a):
    return a * (1.0 / (1.0 + jnp.exp(-a)))


def _gelu(a):
    return 0.5 * a * (1.0 + lax.erf(a * (2.0 ** -0.5)))


def _ada_kernel(c_ref, w_ref, b_ref, o_ref):
    c = c_ref[...]
    ca = _silu(c)
    w = w_ref[0]
    a_hi, a_lo = _split2(ca)
    w_hi, w_lo = _split2(w)
    o_ref[0] = _dot(a_hi, w_hi) + (_dot(a_hi, w_lo) + _dot(a_lo, w_hi)) + b_ref[0]


def _ada_call(c, w_ada, b_ada):
    depth, d, d3 = w_ada.shape
    b = c.shape[0]
    nblk = d3 // d
    return pl.pallas_call(
        _ada_kernel,
        grid=(depth, nblk),
        in_specs=[
            pl.BlockSpec((b, d), lambda l, j: (0, 0)),
            pl.BlockSpec((1, d, d), lambda l, j: (l, 0, j)),
            pl.BlockSpec((1, 1, d), lambda l, j: (l, 0, j)),
        ],
        out_specs=pl.BlockSpec((1, b, d), lambda l, j: (l, 0, j)),
        out_shape=jax.ShapeDtypeStruct((depth, b, d3), F32),
        compiler_params=pltpu.CompilerParams(
            dimension_semantics=("arbitrary", "arbitrary"),
            vmem_limit_bytes=VMEM_LIMIT_BYTES),
        name="adaln_mod",
    )(c, w_ada, b_ada.reshape(depth, 1, d3))


PREP_K = 256


def _prep_kernel(wt_ref, o_ref, oa_ref):
    o_a = 2 * GLA_KW + GLA_WIDTH
    o_r = o_a + GATE_RANK
    piece = 2 * GLA_KW
    for dst in range(0, IN_WIDTH_MAIN, piece):
        src = dst if dst < o_a else dst + GATE_RANK
        o_ref[0, :, dst:dst + piece] = wt_ref[0, src:src + piece, :].T.astype(BF16)
    oa_ref[0] = wt_ref[0, o_a:o_r, :].astype(BF16)


def _prep_call(w_in):
    depth, d, n_in = w_in.shape
    wt = jnp.swapaxes(w_in, 1, 2)
    return pl.pallas_call(
        _prep_kernel,
        grid=(depth, d // PREP_K),
        in_specs=[pl.BlockSpec((1, n_in, PREP_K), lambda l, i: (l, 0, i))],
        out_specs=[pl.BlockSpec((1, PREP_K, IN_WIDTH_MAIN), lambda l, i: (l, i, 0)),
                   pl.BlockSpec((1, GATE_RANK, PREP_K), lambda l, i: (l, 0, i))],
        out_shape=[jax.ShapeDtypeStruct((depth, d, IN_WIDTH_MAIN), BF16),
                   jax.ShapeDtypeStruct((depth, GATE_RANK, d), BF16)],
        compiler_params=pltpu.CompilerParams(
            dimension_semantics=("arbitrary", "arbitrary"),
            vmem_limit_bytes=VMEM_LIMIT_BYTES),
        name="w_in_relayout",
    )(wt)


def _layer_kernel(xc_ref, modc_ref, ng_ref, win_ref, wat_ref, wa2t_ref, ba_ref, gng_ref,
                  lng_ref, lnb_ref, ws_ref, bsp_ref, wout_ref, fg_ref, o_ref,
                  st_ref, yin_ref, *, final, tiles_per_seq):
    g = pl.program_id(0)
    ts = xc_ref.shape[1]
    sub = min(SUB_TILE, ts)
    n_gla_chunks = sub // GLA_CHUNK
    n_sg_chunks = sub // SG_CHUNK
    half_rows = sub // 2
    pairs = range(GLA_HEADS // 2)

    @pl.when(g % tiles_per_seq == 0)
    def _():
        st_ref[...] = jnp.zeros_like(st_ref)

    gng = gng_ref[0]
    lng = lng_ref[0]
    lnb = lnb_ref[0]
    shift = modc_ref[0, 0, 0:1, :]
    scale = modc_ref[0, 0, 1:2, :]
    norm_gain = ng_ref[0] * (1.0 + scale)

    row = lax.broadcasted_iota(jnp.int32, (MXU_DIM, MXU_DIM), 0)
    col = lax.broadcasted_iota(jnp.int32, (MXU_DIM, MXU_DIM), 1)
    same_chunk = (row // GLA_CHUNK) == (col // GLA_CHUNK)
    cum_mat = jnp.where(same_chunk & (col <= row), 1.0, 0.0).astype(BF16)

    pr = lax.broadcasted_iota(jnp.int32, (2 * GLA_CHUNK, LANES), 0)
    pc = lax.broadcasted_iota(jnp.int32, (2 * GLA_CHUNK, LANES), 1)
    own_lanes = (pr // GLA_CHUNK) == (pc // GLA_DK)
    at_ = lax.broadcasted_iota(jnp.int32, (GLA_CHUNK, 2 * GLA_CHUNK), 0)
    as_ = lax.broadcasted_iota(jnp.int32, (GLA_CHUNK, 2 * GLA_CHUNK), 1)
    pair_causal = (as_ % GLA_CHUNK) <= at_

    def stack_heads(a):
        a2 = jnp.concatenate([a, a], axis=0)
        return jnp.where(own_lanes, a2, jnp.zeros_like(a2))

    st_ = lax.broadcasted_iota(jnp.int32, (SG_CHUNK, SG_CHUNK), 0)
    ss_ = lax.broadcasted_iota(jnp.int32, (SG_CHUNK, SG_CHUNK), 1)
    sg_causal = ss_ <= st_
    sg_half = SG_WIDTH // 2

    state = [st_ref[p] for p in pairs]

    def sub_tile(base):
        xx = xc_ref[0, base:base + sub, :]
        ms = jnp.mean(xx * xx, axis=-1, keepdims=True)
        hb = ((xx * lax.rsqrt(ms + EPS)) * norm_gain + shift).astype(BF16)
        yield

        def proj(off, width, rows=slice(None)):
            return _dot(hb[rows], win_ref[0, :, off:off + width])

        qk = proj(_OFF_Q, 2 * GLA_KW)
        q = qk[:, :GLA_KW]
        k = qk[:, GLA_KW:]
        a_lr_t = _dot_nt(wat_ref[0], hb)
        z_t = _dot(wa2t_ref[0].astype(F32), a_lr_t.astype(BF16).astype(F32))
        z = z_t.T + ba_ref[0]
        log_a = (jnp.minimum(z, 0.0) - jnp.log(1.0 + jnp.exp(-jnp.abs(z)))) * (1.0 / GATE_TAU)
        v_g = proj(_OFF_VG, GLA_WIDTH).astype(BF16)

        la_parts = _split2(log_a)
        g_cum = jnp.concatenate(
            [sum(_dot(cum_mat, part[r:r + MXU_DIM]) for part in la_parts)
             for r in range(0, sub, MXU_DIM)], axis=0)
        r_g = _silu(proj(_OFF_RG, GLA_WIDTH))

        q_t = (q * jnp.exp(g_cum) * (GLA_DK ** -0.5)).astype(BF16)
        k_t = (k * jnp.exp(-g_cum)).astype(BF16)

        last_rows = jnp.concatenate(
            [g_cum[(c + 1) * GLA_CHUNK - 1:(c + 1) * GLA_CHUNK, :] for c in range(n_gla_chunks)],
            axis=0)
        dec_cols = jnp.exp(last_rows).T

        att = {}
        o_rows = [[None] * len(pairs) for _ in range(n_gla_chunks)]

        def gla_scores(c):
            rows = slice(c * GLA_CHUNK, (c + 1) * GLA_CHUNK)
            for p in pairs:
                lanes = slice(p * LANES, (p + 1) * LANES)
                s = _dot_nt(q_t[rows, lanes], stack_heads(k_t[rows, lanes]))
                att[c, p] = jnp.where(pair_causal, s, 0.0).astype(BF16)

        def gla_apply(c):
            rows = slice(c * GLA_CHUNK, (c + 1) * GLA_CHUNK)
            k_h = (k[rows] * jnp.exp(last_rows[c:c + 1, :] - g_cum[rows])).astype(BF16)
            for p in pairs:
                lanes = slice(p * LANES, (p + 1) * LANES)
                v0 = v_g[rows, (2 * p) * GLA_DV:(2 * p + 1) * GLA_DV]
                v1 = v_g[rows, (2 * p + 1) * GLA_DV:(2 * p + 2) * GLA_DV]
                zero = jnp.zeros_like(v0)
                s_b = state[p].astype(BF16)
                rhs = jnp.concatenate(
                    [jnp.concatenate([v0, zero], axis=1),
                     jnp.concatenate([zero, v1], axis=1),
                     jnp.concatenate([s_b[:GLA_DK], zero], axis=1),
                     jnp.concatenate([zero, s_b[GLA_DK:]], axis=1)], axis=0)
                lhs = jnp.concatenate([att.pop((c, p)), q_t[rows, lanes]], axis=1)
                o_rows[c][p] = _dot(lhs, rhs)
                state[p] = (dec_cols[lanes, c:c + 1] * state[p]
                            + _dot_tn(stack_heads(k_h[:, lanes]),
                                      jnp.concatenate([v0, v1], axis=0)))

        acts = {}

        def sg_proj(name, off, act, hf):
            acts[name, hf] = act(proj(off + hf * sg_half, sg_half))

        def sg_group(grp):
            hf, sl = divmod(grp * SG_GROUP_DIM, sg_half)
            sl = slice(sl, sl + SG_GROUP_DIM)
            gl = slice(grp * SG_GROUP_DIM, (grp + 1) * SG_GROUP_DIM)
            v_grp = acts["v", hf][:, sl]
            mu = jnp.mean(v_grp, axis=-1, keepdims=True)
            cen = v_grp - mu
            var = jnp.mean(cen * cen, axis=-1, keepdims=True)
            vn = (cen * lax.rsqrt(var + EPS) * lng[:, gl] + lnb[:, gl]).astype(BF16)
            w_g = jnp.where(sg_causal, ws_ref[0, grp], 0.0).astype(BF16)
            vn_wide = jnp.concatenate(
                [vn[n * SG_CHUNK:(n + 1) * SG_CHUNK, :] for n in range(n_sg_chunks)], axis=1)
            mixed_wide = _dot(w_g, vn_wide)
            bias = bsp_ref[0, :, gl]
            for n in range(n_sg_chunks):
                rs_ = slice(n * SG_CHUNK, (n + 1) * SG_CHUNK)
                mixed = mixed_wide[:, n * SG_GROUP_DIM:(n + 1) * SG_GROUP_DIM] + bias
                out = acts["u", hf][rs_, sl] * mixed * acts["r", hf][rs_, sl]
                yin_ref[base + n * SG_CHUNK:base + (n + 1) * SG_CHUNK,
                        GLA_WIDTH + grp * SG_GROUP_DIM:GLA_WIDTH + (grp + 1) * SG_GROUP_DIM] = (
                    out.astype(BF16))

        def finish_rows(rh, n_split):
            rows = slice(rh * half_rows, (rh + 1) * half_rows)
            out_rows = slice(base + rh * half_rows, base + (rh + 1) * half_rows)
            chunks = range(rh * n_gla_chunks // 2, (rh + 1) * n_gla_chunks // 2)
            for hd in range(GLA_HEADS):
                sl = slice(hd * GLA_DV, (hd + 1) * GLA_DV)
                o_h = jnp.concatenate(
                    [o_rows[c][hd // 2][:, (hd % 2) * GLA_DV:(hd % 2 + 1) * GLA_DV]
                     for c in chunks], axis=0)
                o_h = (o_h * lax.rsqrt(jnp.mean(o_h * o_h, axis=-1, keepdims=True) + EPS)
                       * gng[:, sl])
                yin_ref[out_rows, sl] = (o_h * r_g[rows, sl]).astype(BF16)
            width = D_MODEL // n_split
            parts = []
            for nh in range(n_split):
                cols = slice(nh * width, (nh + 1) * width)
                y = _dot(yin_ref[out_rows, :], wout_ref[0, :, cols])
                parts.append(xc_ref[0, out_rows, cols] + modc_ref[0, 0, 2:3, cols] * y)
            if final:
                ms2 = sum(jnp.sum(part * part, axis=-1, keepdims=True) for part in parts) / D_MODEL
                scale_rows = lax.rsqrt(ms2 + EPS)
                parts = [part * scale_rows * fg_ref[:, nh * width:(nh + 1) * width]
                         for nh, part in enumerate(parts)]
            for nh, part in enumerate(parts):
                o_ref[0, out_rows, nh * width:(nh + 1) * width] = part

        fillers = [
            functools.partial(sg_proj, "v", _OFF_VS, _gelu, 0),
            functools.partial(sg_proj, "u", _OFF_US, _gelu, 0),
            functools.partial(sg_proj, "r", _OFF_RS, _silu, 0),
            functools.partial(sg_proj, "v", _OFF_VS, _gelu, 1),
            lambda: (sg_group(0), sg_group(1), sg_proj("u", _OFF_US, _gelu, 1)),
            functools.partial(sg_proj, "r", _OFF_RS, _silu, 1),
            lambda: (sg_group(2), sg_group(3)),
        ]
        gla_scores(0)
        for c in range(n_gla_chunks):
            if c + 1 < n_gla_chunks:
                gla_scores(c + 1)
            for _ in range(-(-len(fillers) // (n_gla_chunks - c))):
                fillers.pop(0)()
            if c == n_gla_chunks - 1:
                finish_rows(0, 1)
            gla_apply(c)
        yield
        finish_rows(1, 2)

    tiles = [sub_tile(base) for base in range(0, ts, sub)]
    next(tiles[0])
    for i, tile in enumerate(tiles):
        next(tile)
        if i + 1 < len(tiles):
            next(tiles[i + 1])
        for _ in tile:
            pass
    for p in pairs:
        st_ref[p] = state[p]


def _layer_call(layer, x, mod, ng, win, wat, wa2t, ba, gng, lng, lnb, ws, bsp, wout, fg, *, final):
    b, s, d = x.shape
    ts = SEQ_TILE
    tiles_per_seq = s // ts
    n_tiles = b * tiles_per_seq
    xt = x.reshape(n_tiles, ts, d)
    lay3 = lambda g: (layer, 0, 0)
    lay4 = lambda g: (layer, 0, 0, 0)
    out = pl.pallas_call(
        functools.partial(_layer_kernel, final=final, tiles_per_seq=tiles_per_seq),
        grid=(n_tiles,),
        in_specs=[
            pl.BlockSpec((1, ts, d), lambda g: (g, 0, 0)),
            pl.BlockSpec((1, 1, 3, d), lambda g: (layer, g // tiles_per_seq, 0, 0)),
            pl.BlockSpec((1, 1, d), lay3),
            pl.BlockSpec((1, d, IN_WIDTH_MAIN), lay3),
            pl.BlockSpec((1, GATE_RANK, d), lay3),
            pl.BlockSpec((1, GLA_KW, GATE_RANK), lay3),
            pl.BlockSpec((1, 1, GLA_KW), lay3),
            pl.BlockSpec((1, 1, GLA_WIDTH), lay3),
            pl.BlockSpec((1, 1, SG_WIDTH), lay3),
            pl.BlockSpec((1, 1, SG_WIDTH), lay3),
            pl.BlockSpec((1, SG_GROUPS, SG_CHUNK, SG_CHUNK), lay4),
            pl.BlockSpec((1, SG_CHUNK, SG_WIDTH), lay3),
            pl.BlockSpec((1, d, d), lay3),
            pl.BlockSpec((1, d), lambda g: (0, 0)),
        ],
        out_specs=pl.BlockSpec((1, ts, d), lambda g: (g, 0, 0)),
        out_shape=jax.ShapeDtypeStruct((n_tiles, ts, d), F32),
        scratch_shapes=[
            pltpu.VMEM((GLA_HEADS // 2, 2 * GLA_DK, GLA_DV), F32),
            pltpu.VMEM((ts, d), BF16),
        ],
        compiler_params=pltpu.CompilerParams(
            dimension_semantics=("arbitrary",),
            vmem_limit_bytes=VMEM_LIMIT_BYTES),
        name="gla_gmlp_layer",
    )(xt, mod, ng, win, wat, wa2t, ba, gng, lng, lnb, ws, bsp, wout, fg)
    return out.reshape(b, s, d)


def kernel(x, c, norm_g, w_ada, b_ada, w_in, w_alpha2, b_alpha, gla_norm_g, sg_ln_g, sg_ln_b,
           w_spatial, b_spatial, w_out, final_g):
    depth = w_in.shape[0]
    bsz = x.shape[0]
    d = x.shape[-1]

    mod = _ada_call(c, w_ada, b_ada).reshape(depth, bsz, 3, d)
    win, wat = _prep_call(w_in)
    wa2t = jnp.swapaxes(w_alpha2, 1, 2).astype(BF16)
    wout = w_out.astype(BF16)
    bsp = jnp.repeat(jnp.transpose(b_spatial, (0, 2, 1)), SG_GROUP_DIM, axis=-1)
    row3 = lambda a: a[:, None, :]

    for l in range(depth):
        x = _layer_call(
            l, x, mod, row3(norm_g), win, wat, wa2t, row3(b_alpha), row3(gla_norm_g), row3(sg_ln_g),
            row3(sg_ln_b), w_spatial, bsp, wout, final_g[None], final=(l == depth - 1))
    return x
```

```python
import functools

import jax
import jax.numpy as jnp
from jax import lax
from jax.experimental import pallas as pl
from jax.experimental.pallas import tpu as pltpu

F32 = jnp.float32
BF16 = jnp.bfloat16

D_MODEL = 1024
GLA_HEADS = 4
GLA_DK = 64
GLA_DV = 128
GLA_KW = GLA_HEADS * GLA_DK
GLA_WIDTH = GLA_HEADS * GLA_DV
GATE_RANK = 16
GATE_TAU = 16.0
GLA_CHUNK = 64
SG_GROUPS = 4
SG_GROUP_DIM = 128
SG_WIDTH = SG_GROUPS * SG_GROUP_DIM
SG_CHUNK = 128
EPS = 1e-6

LANES = 128
MXU_DIM = 256
RANK_PAD = LANES
SEQ_TILE = 1024
SUB_TILE = 512
VMEM_LIMIT_BYTES = 48 * 1024 * 1024

_OFF_Q = 0
_OFF_K = _OFF_Q + GLA_KW
_OFF_A = _OFF_K + GLA_KW
_OFF_VG = _OFF_A + RANK_PAD
_OFF_RG = _OFF_VG + GLA_WIDTH
_OFF_US = _OFF_RG + GLA_WIDTH
_OFF_VS = _OFF_US + SG_WIDTH
_OFF_RS = _OFF_VS + SG_WIDTH
IN_WIDTH_PAD = _OFF_RS + SG_WIDTH


def _split2(a):
    hi = a.astype(BF16)
    return hi, (a - hi.astype(F32)).astype(BF16)


def _dot(a, b):
    return jnp.dot(a, b, preferred_element_type=F32)


def _dot_nt(a, b):
    return lax.dot_general(a, b, (((1,), (1,)), ((), ())), preferred_element_type=F32)


def _dot_tn(a, b):
    return lax.dot_general(a, b, (((0,), (0,)), ((), ())), preferred_element_type=F32)


def _silu(a):
    return a * (1.0 / (1.0 + jnp.exp(-a)))


def _gelu(a):
    return 0.5 * a * (1.0 + lax.erf(a * (2.0 ** -0.5)))


def _ada_kernel(c_ref, w_ref, b_ref, o_ref):
    c = c_ref[...]
    ca = _silu(c)
    w = w_ref[0]
    a_hi, a_lo = _split2(ca)
    w_hi, w_lo = _split2(w)
    o_ref[0] = _dot(a_hi, w_hi) + (_dot(a_hi, w_lo) + _dot(a_lo, w_hi)) + b_ref[0]


def _ada_call(c, w_ada, b_ada):
    depth, d, d3 = w_ada.shape
    b = c.shape[0]
    nblk = d3 // d
    return pl.pallas_call(
        _ada_kernel,
        grid=(depth, nblk),
        in_specs=[
            pl.BlockSpec((b, d), lambda l, j: (0, 0)),
            pl.BlockSpec((1, d, d), lambda l, j: (l, 0, j)),
            pl.BlockSpec((1, 1, d), lambda l, j: (l, 0, j)),
        ],
        out_specs=pl.BlockSpec((1, b, d), lambda l, j: (l, 0, j)),
        out_shape=jax.ShapeDtypeStruct((depth, b, d3), F32),
        compiler_params=pltpu.CompilerParams(
            dimension_semantics=("arbitrary", "arbitrary"),
            vmem_limit_bytes=VMEM_LIMIT_BYTES),
        name="adaln_mod",
    )(c, w_ada, b_ada.reshape(depth, 1, d3))


PREP_K = 256


def _prep_kernel(wt_ref, o_ref):
    o_v = 2 * GLA_KW
    o_a = o_v + GLA_WIDTH
    o_r = o_a + GATE_RANK
    piece = 2 * GLA_KW

    def put(dst, src, width=piece):
        o_ref[0, :, dst:dst + width] = wt_ref[0, src:src + width, :].T.astype(BF16)

    put(_OFF_Q, 0)
    a_tile = wt_ref[0, o_a:o_a + RANK_PAD, :].T
    lane = lax.broadcasted_iota(jnp.int32, a_tile.shape, 1)
    o_ref[0, :, _OFF_A:_OFF_VG] = jnp.where(lane < GATE_RANK, a_tile, 0.0).astype(BF16)
    put(_OFF_VG, o_v)
    for i in range((IN_WIDTH_PAD - _OFF_RG) // piece):
        put(_OFF_RG + i * piece, o_r + i * piece)


def _prep_call(w_in):
    depth, d, n_in = w_in.shape
    wt = jnp.swapaxes(w_in, 1, 2)
    return pl.pallas_call(
        _prep_kernel,
        grid=(depth, d // PREP_K),
        in_specs=[pl.BlockSpec((1, n_in, PREP_K), lambda l, i: (l, 0, i))],
        out_specs=pl.BlockSpec((1, PREP_K, IN_WIDTH_PAD), lambda l, i: (l, i, 0)),
        out_shape=jax.ShapeDtypeStruct((depth, d, IN_WIDTH_PAD), BF16),
        compiler_params=pltpu.CompilerParams(
            dimension_semantics=("arbitrary", "arbitrary"),
            vmem_limit_bytes=VMEM_LIMIT_BYTES),
        name="w_in_relayout",
    )(wt)


def _layer_kernel(xc_ref, modc_ref, ng_ref, win_ref, wa2_ref, ba_ref, gng_ref,
                  lng_ref, lnb_ref, ws_ref, bsp_ref, wout_ref, fg_ref, o_ref,
                  st_ref, yin_ref, *, final, tiles_per_seq):
    g = pl.program_id(0)
    ts = xc_ref.shape[1]
    sub = min(SUB_TILE, ts)
    n_gla_chunks = sub // GLA_CHUNK
    n_sg_chunks = sub // SG_CHUNK
    half_rows = sub // 2
    pairs = range(GLA_HEADS // 2)

    @pl.when(g % tiles_per_seq == 0)
    def _():
        st_ref[...] = jnp.zeros_like(st_ref)

    gng = gng_ref[0]
    lng = lng_ref[0]
    lnb = lnb_ref[0]
    shift = modc_ref[0, 0, 0:1, :]
    scale = modc_ref[0, 0, 1:2, :]
    norm_gain = ng_ref[0] * (1.0 + scale)

    row = lax.broadcasted_iota(jnp.int32, (MXU_DIM, MXU_DIM), 0)
    col = lax.broadcasted_iota(jnp.int32, (MXU_DIM, MXU_DIM), 1)
    same_chunk = (row // GLA_CHUNK) == (col // GLA_CHUNK)
    cum_mat = jnp.where(same_chunk & (col <= row), 1.0, 0.0).astype(BF16)

    pr = lax.broadcasted_iota(jnp.int32, (2 * GLA_CHUNK, LANES), 0)
    pc = lax.broadcasted_iota(jnp.int32, (2 * GLA_CHUNK, LANES), 1)
    own_lanes = (pr // GLA_CHUNK) == (pc // GLA_DK)
    at_ = lax.broadcasted_iota(jnp.int32, (GLA_CHUNK, 2 * GLA_CHUNK), 0)
    as_ = lax.broadcasted_iota(jnp.int32, (GLA_CHUNK, 2 * GLA_CHUNK), 1)
    pair_causal = (as_ % GLA_CHUNK) <= at_

    def stack_heads(a):
        a2 = jnp.concatenate([a, a], axis=0)
        return jnp.where(own_lanes, a2, jnp.zeros_like(a2))

    st_ = lax.broadcasted_iota(jnp.int32, (SG_CHUNK, SG_CHUNK), 0)
    ss_ = lax.broadcasted_iota(jnp.int32, (SG_CHUNK, SG_CHUNK), 1)
    sg_causal = ss_ <= st_
    sg_half = SG_WIDTH // 2

    state = [st_ref[p] for p in pairs]

    def sub_tile(base):
        xx = xc_ref[0, base:base + sub, :]
        ms = jnp.mean(xx * xx, axis=-1, keepdims=True)
        hb = ((xx * lax.rsqrt(ms + EPS)) * norm_gain + shift).astype(BF16)
        yield

        def proj(off, width, rows=slice(None)):
            return _dot(hb[rows], win_ref[0, :, off:off + width])

        a_lr = jnp.concatenate(
            [proj(_OFF_A, RANK_PAD, slice(0, half_rows)),
             proj(_OFF_A, RANK_PAD, slice(half_rows, sub))], axis=0)
        z = _dot(a_lr.astype(BF16), wa2_ref[0]) + ba_ref[0]
        qk = proj(_OFF_Q, 2 * GLA_KW)
        q = qk[:, :GLA_KW]
        k = qk[:, GLA_KW:]
        log_a = (jnp.minimum(z, 0.0) - jnp.log(1.0 + jnp.exp(-jnp.abs(z)))) * (1.0 / GATE_TAU)
        v_g = proj(_OFF_VG, GLA_WIDTH).astype(BF16)

        la_parts = _split2(log_a)
        g_cum = jnp.concatenate(
            [sum(_dot(cum_mat, part[r:r + MXU_DIM]) for part in la_parts)
             for r in range(0, sub, MXU_DIM)], axis=0)
        r_g = _silu(proj(_OFF_RG, GLA_WIDTH))

        q_t = (q * jnp.exp(g_cum) * (GLA_DK ** -0.5)).astype(BF16)
        k_t = (k * jnp.exp(-g_cum)).astype(BF16)

        last_rows = jnp.concatenate(
            [g_cum[(c + 1) * GLA_CHUNK - 1:(c + 1) * GLA_CHUNK, :] for c in range(n_gla_chunks)],
            axis=0)
        dec_cols = jnp.exp(last_rows).T

        att = {}
        o_rows = [[None] * len(pairs) for _ in range(n_gla_chunks)]

        def gla_scores(c):
            rows = slice(c * GLA_CHUNK, (c + 1) * GLA_CHUNK)
            for p in pairs:
                lanes = slice(p * LANES, (p + 1) * LANES)
                s = _dot_nt(q_t[rows, lanes], stack_heads(k_t[rows, lanes]))
                att[c, p] = jnp.where(pair_causal, s, 0.0).astype(BF16)

        def gla_apply(c):
            rows = slice(c * GLA_CHUNK, (c + 1) * GLA_CHUNK)
            k_h = (k[rows] * jnp.exp(last_rows[c:c + 1, :] - g_cum[rows])).astype(BF16)
            for p in pairs:
                lanes = slice(p * LANES, (p + 1) * LANES)
                v0 = v_g[rows, (2 * p) * GLA_DV:(2 * p + 1) * GLA_DV]
                v1 = v_g[rows, (2 * p + 1) * GLA_DV:(2 * p + 2) * GLA_DV]
                zero = jnp.zeros_like(v0)
                s_b = state[p].astype(BF16)
                rhs = jnp.concatenate(
                    [jnp.concatenate([v0, zero], axis=1),
                     jnp.concatenate([zero, v1], axis=1),
                     jnp.concatenate([s_b[:GLA_DK], zero], axis=1),
                     jnp.concatenate([zero, s_b[GLA_DK:]], axis=1)], axis=0)
                lhs = jnp.concatenate([att.pop((c, p)), q_t[rows, lanes]], axis=1)
                o_rows[c][p] = _dot(lhs, rhs)
                state[p] = (dec_cols[lanes, c:c + 1] * state[p]
                            + _dot_tn(stack_heads(k_h[:, lanes]),
                                      jnp.concatenate([v0, v1], axis=0)))

        acts = {}

        def sg_proj(name, off, act, hf):
            acts[name, hf] = act(proj(off + hf * sg_half, sg_half))

        def sg_group(grp):
            hf, sl = divmod(grp * SG_GROUP_DIM, sg_half)
            sl = slice(sl, sl + SG_GROUP_DIM)
            gl = slice(grp * SG_GROUP_DIM, (grp + 1) * SG_GROUP_DIM)
            v_grp = acts["v", hf][:, sl]
            mu = jnp.mean(v_grp, axis=-1, keepdims=True)
            cen = v_grp - mu
            var = jnp.mean(cen * cen, axis=-1, keepdims=True)
            vn = (cen * lax.rsqrt(var + EPS) * lng[:, gl] + lnb[:, gl]).astype(BF16)
            w_g = jnp.where(sg_causal, ws_ref[0, grp], 0.0).astype(BF16)
            vn_wide = jnp.concatenate(
                [vn[n * SG_CHUNK:(n + 1) * SG_CHUNK, :] for n in range(n_sg_chunks)], axis=1)
            mixed_wide = _dot(w_g, vn_wide)
            bias = bsp_ref[0, :, gl]
            for n in range(n_sg_chunks):
                rs_ = slice(n * SG_CHUNK, (n + 1) * SG_CHUNK)
                mixed = mixed_wide[:, n * SG_GROUP_DIM:(n + 1) * SG_GROUP_DIM] + bias
                out = acts["u", hf][rs_, sl] * mixed * acts["r", hf][rs_, sl]
                yin_ref[base + n * SG_CHUNK:base + (n + 1) * SG_CHUNK,
                        GLA_WIDTH + grp * SG_GROUP_DIM:GLA_WIDTH + (grp + 1) * SG_GROUP_DIM] = (
                    out.astype(BF16))

        def finish_rows(rh, n_split):
            rows = slice(rh * half_rows, (rh + 1) * half_rows)
            out_rows = slice(base + rh * half_rows, base + (rh + 1) * half_rows)
            chunks = range(rh * n_gla_chunks // 2, (rh + 1) * n_gla_chunks // 2)
            for hd in range(GLA_HEADS):
                sl = slice(hd * GLA_DV, (hd + 1) * GLA_DV)
                o_h = jnp.concatenate(
                    [o_rows[c][hd // 2][:, (hd % 2) * GLA_DV:(hd % 2 + 1) * GLA_DV]
                     for c in chunks], axis=0)
                o_h = (o_h * lax.rsqrt(jnp.mean(o_h * o_h, axis=-1, keepdims=True) + EPS)
                       * gng[:, sl])
                yin_ref[out_rows, sl] = (o_h * r_g[rows, sl]).astype(BF16)
            width = D_MODEL // n_split
            parts = []
            for nh in range(n_split):
                cols = slice(nh * width, (nh + 1) * width)
                y = _dot(yin_ref[out_rows, :], wout_ref[0, :, cols])
                parts.append(xc_ref[0, out_rows, cols] + modc_ref[0, 0, 2:3, cols] * y)
            if final:
                ms2 = sum(jnp.sum(part * part, axis=-1, keepdims=True) for part in parts) / D_MODEL
                scale_rows = lax.rsqrt(ms2 + EPS)
                parts = [part * scale_rows * fg_ref[:, nh * width:(nh + 1) * width]
                         for nh, part in enumerate(parts)]
            for nh, part in enumerate(parts):
                o_ref[0, out_rows, nh * width:(nh + 1) * width] = part

        fillers = [
            functools.partial(sg_proj, "v", _OFF_VS, _gelu, 0),
            functools.partial(sg_proj, "u", _OFF_US, _gelu, 0),
            functools.partial(sg_proj, "r", _OFF_RS, _silu, 0),
            functools.partial(sg_proj, "v", _OFF_VS, _gelu, 1),
            lambda: (sg_group(0), sg_group(1), sg_proj("u", _OFF_US, _gelu, 1)),
            functools.partial(sg_proj, "r", _OFF_RS, _silu, 1),
            lambda: (sg_group(2), sg_group(3)),
        ]
        gla_scores(0)
        for c in range(n_gla_chunks):
            if c + 1 < n_gla_chunks:
                gla_scores(c + 1)
            for _ in range(-(-len(fillers) // (n_gla_chunks - c))):
                fillers.pop(0)()
            if c == n_gla_chunks - 1:
                finish_rows(0, 1)
            gla_apply(c)
        yield
        finish_rows(1, 2)

    tiles = [sub_tile(base) for base in range(0, ts, sub)]
    next(tiles[0])
    for i, tile in enumerate(tiles):
        next(tile)
        if i + 1 < len(tiles):
            next(tiles[i + 1])
        for _ in tile:
            pass
    for p in pairs:
        st_ref[p] = state[p]


def _layer_call(layer, x, mod, ng, win, wa2, ba, gng, lng, lnb, ws, bsp, wout, fg, *, final):
    b, s, d = x.shape
    ts = SEQ_TILE
    tiles_per_seq = s // ts
    n_tiles = b * tiles_per_seq
    xt = x.reshape(n_tiles, ts, d)
    lay3 = lambda g: (layer, 0, 0)
    lay4 = lambda g: (layer, 0, 0, 0)
    out = pl.pallas_call(
        functools.partial(_layer_kernel, final=final, tiles_per_seq=tiles_per_seq),
        grid=(n_tiles,),
        in_specs=[
            pl.BlockSpec((1, ts, d), lambda g: (g, 0, 0)),
            pl.BlockSpec((1, 1, 3, d), lambda g: (layer, g // tiles_per_seq, 0, 0)),
            pl.BlockSpec((1, 1, d), lay3),
            pl.BlockSpec((1, d, IN_WIDTH_PAD), lay3),
            pl.BlockSpec((1, RANK_PAD, GLA_KW), lay3),
            pl.BlockSpec((1, 1, GLA_KW), lay3),
            pl.BlockSpec((1, 1, GLA_WIDTH), lay3),
            pl.BlockSpec((1, 1, SG_WIDTH), lay3),
            pl.BlockSpec((1, 1, SG_WIDTH), lay3),
            pl.BlockSpec((1, SG_GROUPS, SG_CHUNK, SG_CHUNK), lay4),
            pl.BlockSpec((1, SG_CHUNK, SG_WIDTH), lay3),
            pl.BlockSpec((1, d, d), lay3),
            pl.BlockSpec((1, d), lambda g: (0, 0)),
        ],
        out_specs=pl.BlockSpec((1, ts, d), lambda g: (g, 0, 0)),
        out_shape=jax.ShapeDtypeStruct((n_tiles, ts, d), F32),
        scratch_shapes=[
            pltpu.VMEM((GLA_HEADS // 2, 2 * GLA_DK, GLA_DV), F32),
            pltpu.VMEM((ts, d), BF16),
        ],
        compiler_params=pltpu.CompilerParams(
            dimension_semantics=("arbitrary",),
            vmem_limit_bytes=VMEM_LIMIT_BYTES),
        name="gla_gmlp_layer",
    )(xt, mod, ng, win, wa2, ba, gng, lng, lnb, ws, bsp, wout, fg)
    return out.reshape(b, s, d)


def kernel(x, c, norm_g, w_ada, b_ada, w_in, w_alpha2, b_alpha, gla_norm_g, sg_ln_g, sg_ln_b,
           w_spatial, b_spatial, w_out, final_g):
    depth = w_in.shape[0]
    bsz = x.shape[0]
    d = x.shape[-1]

    mod = _ada_call(c, w_ada, b_ada).reshape(depth, bsz, 3, d)
    win = _prep_call(w_in)
    wa2 = jnp.pad(w_alpha2, ((0, 0), (0, RANK_PAD - GATE_RANK), (0, 0))).astype(BF16)
    wout = w_out.astype(BF16)
    bsp = jnp.repeat(jnp.transpose(b_spatial, (0, 2, 1)), SG_GROUP_DIM, axis=-1)
    row3 = lambda a: a[:, None, :]

    for l in range(depth):
        x = _layer_call(
            l, x, mod, row3(norm_g), win, wa2, row3(b_alpha), row3(gla_norm_g), row3(sg_ln_g),
            row3(sg_ln_b), w_spatial, bsp, wout, final_g[None], final=(l == depth - 1))
    return x
```

```python
import functools

import jax
import jax.numpy as jnp
from jax import lax
from jax.experimental import pallas as pl
from jax.experimental.pallas import tpu as pltpu

F32 = jnp.float32
BF16 = jnp.bfloat16

D_MODEL = 1024
GLA_HEADS = 4
GLA_DK = 64
GLA_DV = 128
GLA_KW = GLA_HEADS * GLA_DK
GLA_WIDTH = GLA_HEADS * GLA_DV
GATE_RANK = 16
GATE_TAU = 16.0
GLA_CHUNK = 64
SG_GROUPS = 4
SG_GROUP_DIM = 128
SG_WIDTH = SG_GROUPS * SG_GROUP_DIM
SG_CHUNK = 128
EPS = 1e-6

LANES = 128
MXU_DIM = 256
RANK_PAD = LANES
SEQ_TILE = 512
SUB_TILE = 512
VMEM_LIMIT_BYTES = 48 * 1024 * 1024

_OFF_Q = 0
_OFF_K = _OFF_Q + GLA_KW
_OFF_A = _OFF_K + GLA_KW
_OFF_VG = _OFF_A + RANK_PAD
_OFF_RG = _OFF_VG + GLA_WIDTH
_OFF_US = _OFF_RG + GLA_WIDTH
_OFF_VS = _OFF_US + SG_WIDTH
_OFF_RS = _OFF_VS + SG_WIDTH
IN_WIDTH_PAD = _OFF_RS + SG_WIDTH


def _split2(a):
    hi = a.astype(BF16)
    return hi, (a - hi.astype(F32)).astype(BF16)


def _dot(a, b):
    return jnp.dot(a, b, preferred_element_type=F32)


def _dot_nt(a, b):
    return lax.dot_general(a, b, (((1,), (1,)), ((), ())), preferred_element_type=F32)


def _dot_tn(a, b):
    return lax.dot_general(a, b, (((0,), (0,)), ((), ())), preferred_element_type=F32)


def _silu(a):
    return a * (1.0 / (1.0 + jnp.exp(-a)))


def _gelu(a):
    return 0.5 * a * (1.0 + lax.erf(a * (2.0 ** -0.5)))


def _ada_kernel(c_ref, w_ref, b_ref, o_ref):
    c = c_ref[...]
    ca = _silu(c)
    w = w_ref[0]
    a_hi, a_lo = _split2(ca)
    w_hi, w_lo = _split2(w)
    o_ref[0] = _dot(a_hi, w_hi) + (_dot(a_hi, w_lo) + _dot(a_lo, w_hi)) + b_ref[0]


def _ada_call(c, w_ada, b_ada):
    depth, d, d3 = w_ada.shape
    b = c.shape[0]
    nblk = d3 // d
    return pl.pallas_call(
        _ada_kernel,
        grid=(depth, nblk),
        in_specs=[
            pl.BlockSpec((b, d), lambda l, j: (0, 0)),
            pl.BlockSpec((1, d, d), lambda l, j: (l, 0, j)),
            pl.BlockSpec((1, 1, d), lambda l, j: (l, 0, j)),
        ],
        out_specs=pl.BlockSpec((1, b, d), lambda l, j: (l, 0, j)),
        out_shape=jax.ShapeDtypeStruct((depth, b, d3), F32),
        compiler_params=pltpu.CompilerParams(
            dimension_semantics=("arbitrary", "arbitrary"),
            vmem_limit_bytes=VMEM_LIMIT_BYTES),
        name="adaln_mod",
    )(c, w_ada, b_ada.reshape(depth, 1, d3))


PREP_K = 256


def _prep_kernel(wt_ref, o_ref):
    o_v = 2 * GLA_KW
    o_a = o_v + GLA_WIDTH
    o_r = o_a + GATE_RANK
    piece = 2 * GLA_KW

    def put(dst, src, width=piece):
        o_ref[0, :, dst:dst + width] = wt_ref[0, src:src + width, :].T.astype(BF16)

    put(_OFF_Q, 0)
    a_tile = wt_ref[0, o_a:o_a + RANK_PAD, :].T
    lane = lax.broadcasted_iota(jnp.int32, a_tile.shape, 1)
    o_ref[0, :, _OFF_A:_OFF_VG] = jnp.where(lane < GATE_RANK, a_tile, 0.0).astype(BF16)
    put(_OFF_VG, o_v)
    for i in range((IN_WIDTH_PAD - _OFF_RG) // piece):
        put(_OFF_RG + i * piece, o_r + i * piece)


def _prep_call(w_in):
    depth, d, n_in = w_in.shape
    wt = jnp.swapaxes(w_in, 1, 2)
    return pl.pallas_call(
        _prep_kernel,
        grid=(depth, d // PREP_K),
        in_specs=[pl.BlockSpec((1, n_in, PREP_K), lambda l, i: (l, 0, i))],
        out_specs=pl.BlockSpec((1, PREP_K, IN_WIDTH_PAD), lambda l, i: (l, i, 0)),
        out_shape=jax.ShapeDtypeStruct((depth, d, IN_WIDTH_PAD), BF16),
        compiler_params=pltpu.CompilerParams(
            dimension_semantics=("arbitrary", "arbitrary"),
            vmem_limit_bytes=VMEM_LIMIT_BYTES),
        name="w_in_relayout",
    )(wt)


def _layer_kernel(xc_ref, modc_ref, ng_ref, win_ref, wa2_ref, ba_ref, gng_ref,
                  lng_ref, lnb_ref, ws_ref, bsp_ref, wout_ref, fg_ref, o_ref,
                  st_ref, yin_ref, *, final, tiles_per_seq):
    g = pl.program_id(0)
    ts = xc_ref.shape[1]
    sub = min(SUB_TILE, ts)
    n_gla_chunks = sub // GLA_CHUNK
    n_sg_chunks = sub // SG_CHUNK
    half_rows = sub // 2
    pairs = range(GLA_HEADS // 2)

    @pl.when(g % tiles_per_seq == 0)
    def _():
        st_ref[...] = jnp.zeros_like(st_ref)

    gng = gng_ref[0]
    lng = lng_ref[0]
    lnb = lnb_ref[0]
    shift = modc_ref[0, 0, 0:1, :]
    scale = modc_ref[0, 0, 1:2, :]
    norm_gain = ng_ref[0] * (1.0 + scale)

    row = lax.broadcasted_iota(jnp.int32, (MXU_DIM, MXU_DIM), 0)
    col = lax.broadcasted_iota(jnp.int32, (MXU_DIM, MXU_DIM), 1)
    same_chunk = (row // GLA_CHUNK) == (col // GLA_CHUNK)
    cum_mat = jnp.where(same_chunk & (col <= row), 1.0, 0.0).astype(BF16)

    pr = lax.broadcasted_iota(jnp.int32, (2 * GLA_CHUNK, LANES), 0)
    pc = lax.broadcasted_iota(jnp.int32, (2 * GLA_CHUNK, LANES), 1)
    own_lanes = (pr // GLA_CHUNK) == (pc // GLA_DK)
    at_ = lax.broadcasted_iota(jnp.int32, (GLA_CHUNK, 2 * GLA_CHUNK), 0)
    as_ = lax.broadcasted_iota(jnp.int32, (GLA_CHUNK, 2 * GLA_CHUNK), 1)
    pair_causal = (as_ % GLA_CHUNK) <= at_

    def stack_heads(a):
        a2 = jnp.concatenate([a, a], axis=0)
        return jnp.where(own_lanes, a2, jnp.zeros_like(a2))

    st_ = lax.broadcasted_iota(jnp.int32, (SG_CHUNK, SG_CHUNK), 0)
    ss_ = lax.broadcasted_iota(jnp.int32, (SG_CHUNK, SG_CHUNK), 1)
    sg_causal = ss_ <= st_
    sg_half = SG_WIDTH // 2

    state = [st_ref[p] for p in pairs]

    def sub_tile(base):
        xx = xc_ref[0, base:base + sub, :]
        ms = jnp.mean(xx * xx, axis=-1, keepdims=True)
        hb = ((xx * lax.rsqrt(ms + EPS)) * norm_gain + shift).astype(BF16)
        yield

        def proj(off, width, rows=slice(None)):
            return _dot(hb[rows], win_ref[0, :, off:off + width])

        a_lr = jnp.concatenate(
            [proj(_OFF_A, RANK_PAD, slice(0, half_rows)),
             proj(_OFF_A, RANK_PAD, slice(half_rows, sub))], axis=0)
        z = _dot(a_lr.astype(BF16), wa2_ref[0]) + ba_ref[0]
        qk = proj(_OFF_Q, 2 * GLA_KW)
        q = qk[:, :GLA_KW]
        k = qk[:, GLA_KW:]
        log_a = (jnp.minimum(z, 0.0) - jnp.log(1.0 + jnp.exp(-jnp.abs(z)))) * (1.0 / GATE_TAU)
        v_g = proj(_OFF_VG, GLA_WIDTH).astype(BF16)

        la_parts = _split2(log_a)
        g_cum = jnp.concatenate(
            [sum(_dot(cum_mat, part[r:r + MXU_DIM]) for part in la_parts)
             for r in range(0, sub, MXU_DIM)], axis=0)
        r_g = _silu(proj(_OFF_RG, GLA_WIDTH))

        q_t = (q * jnp.exp(g_cum) * (GLA_DK ** -0.5)).astype(BF16)
        k_t = (k * jnp.exp(-g_cum)).astype(BF16)

        last_rows = jnp.concatenate(
            [g_cum[(c + 1) * GLA_CHUNK - 1:(c + 1) * GLA_CHUNK, :] for c in range(n_gla_chunks)],
            axis=0)
        dec_cols = jnp.exp(last_rows).T

        att = {}
        o_rows = [[None] * len(pairs) for _ in range(n_gla_chunks)]

        def gla_scores(c):
            rows = slice(c * GLA_CHUNK, (c + 1) * GLA_CHUNK)
            for p in pairs:
                lanes = slice(p * LANES, (p + 1) * LANES)
                s = _dot_nt(q_t[rows, lanes], stack_heads(k_t[rows, lanes]))
                att[c, p] = jnp.where(pair_causal, s, 0.0).astype(BF16)

        def gla_apply(c):
            rows = slice(c * GLA_CHUNK, (c + 1) * GLA_CHUNK)
            k_h = (k[rows] * jnp.exp(last_rows[c:c + 1, :] - g_cum[rows])).astype(BF16)
            for p in pairs:
                lanes = slice(p * LANES, (p + 1) * LANES)
                v0 = v_g[rows, (2 * p) * GLA_DV:(2 * p + 1) * GLA_DV]
                v1 = v_g[rows, (2 * p + 1) * GLA_DV:(2 * p + 2) * GLA_DV]
                zero = jnp.zeros_like(v0)
                s_b = state[p].astype(BF16)
                rhs = jnp.concatenate(
                    [jnp.concatenate([v0, zero], axis=1),
                     jnp.concatenate([zero, v1], axis=1),
                     jnp.concatenate([s_b[:GLA_DK], zero], axis=1),
                     jnp.concatenate([zero, s_b[GLA_DK:]], axis=1)], axis=0)
                lhs = jnp.concatenate([att.pop((c, p)), q_t[rows, lanes]], axis=1)
                o_rows[c][p] = _dot(lhs, rhs)
                state[p] = (dec_cols[lanes, c:c + 1] * state[p]
                            + _dot_tn(stack_heads(k_h[:, lanes]),
                                      jnp.concatenate([v0, v1], axis=0)))

        acts = {}

        def sg_proj(name, off, act, hf):
            acts[name, hf] = act(proj(off + hf * sg_half, sg_half))

        def sg_group(grp):
            hf, sl = divmod(grp * SG_GROUP_DIM, sg_half)
            sl = slice(sl, sl + SG_GROUP_DIM)
            gl = slice(grp * SG_GROUP_DIM, (grp + 1) * SG_GROUP_DIM)
            v_grp = acts["v", hf][:, sl]
            mu = jnp.mean(v_grp, axis=-1, keepdims=True)
            cen = v_grp - mu
            var = jnp.mean(cen * cen, axis=-1, keepdims=True)
            vn = (cen * lax.rsqrt(var + EPS) * lng[:, gl] + lnb[:, gl]).astype(BF16)
            w_g = jnp.where(sg_causal, ws_ref[0, grp], 0.0).astype(BF16)
            vn_wide = jnp.concatenate(
                [vn[n * SG_CHUNK:(n + 1) * SG_CHUNK, :] for n in range(n_sg_chunks)], axis=1)
            mixed_wide = _dot(w_g, vn_wide)
            bias = bsp_ref[0, :, gl]
            for n in range(n_sg_chunks):
                rs_ = slice(n * SG_CHUNK, (n + 1) * SG_CHUNK)
                mixed = mixed_wide[:, n * SG_GROUP_DIM:(n + 1) * SG_GROUP_DIM] + bias
                out = acts["u", hf][rs_, sl] * mixed * acts["r", hf][rs_, sl]
                yin_ref[base + n * SG_CHUNK:base + (n + 1) * SG_CHUNK,
                        GLA_WIDTH + grp * SG_GROUP_DIM:GLA_WIDTH + (grp + 1) * SG_GROUP_DIM] = (
                    out.astype(BF16))

        def finish_rows(rh, n_split):
            rows = slice(rh * half_rows, (rh + 1) * half_rows)
            out_rows = slice(base + rh * half_rows, base + (rh + 1) * half_rows)
            chunks = range(rh * n_gla_chunks // 2, (rh + 1) * n_gla_chunks // 2)
            for hd in range(GLA_HEADS):
                sl = slice(hd * GLA_DV, (hd + 1) * GLA_DV)
                o_h = jnp.concatenate(
                    [o_rows[c][hd // 2][:, (hd % 2) * GLA_DV:(hd % 2 + 1) * GLA_DV]
                     for c in chunks], axis=0)
                o_h = (o_h * lax.rsqrt(jnp.mean(o_h * o_h, axis=-1, keepdims=True) + EPS)
                       * gng[:, sl])
                yin_ref[out_rows, sl] = (o_h * r_g[rows, sl]).astype(BF16)
            width = D_MODEL // n_split
            parts = []
            for nh in range(n_split):
                cols = slice(nh * width, (nh + 1) * width)
                y = _dot(yin_ref[out_rows, :], wout_ref[0, :, cols])
                parts.append(xc_ref[0, out_rows, cols] + modc_ref[0, 0, 2:3, cols] * y)
            if final:
                ms2 = sum(jnp.sum(part * part, axis=-1, keepdims=True) for part in parts) / D_MODEL
                scale_rows = lax.rsqrt(ms2 + EPS)
                parts = [part * scale_rows * fg_ref[:, nh * width:(nh + 1) * width]
                         for nh, part in enumerate(parts)]
            for nh, part in enumerate(parts):
                o_ref[0, out_rows, nh * width:(nh + 1) * width] = part

        fillers = [
            functools.partial(sg_proj, "v", _OFF_VS, _gelu, 0),
            functools.partial(sg_proj, "u", _OFF_US, _gelu, 0),
            functools.partial(sg_proj, "r", _OFF_RS, _silu, 0),
            functools.partial(sg_proj, "v", _OFF_VS, _gelu, 1),
            lambda: (sg_group(0), sg_group(1), sg_proj("u", _OFF_US, _gelu, 1)),
            functools.partial(sg_proj, "r", _OFF_RS, _silu, 1),
            lambda: (sg_group(2), sg_group(3)),
        ]
        gla_scores(0)
        for c in range(n_gla_chunks):
            if c + 1 < n_gla_chunks:
                gla_scores(c + 1)
            for _ in range(-(-len(fillers) // (n_gla_chunks - c))):
                fillers.pop(0)()
            if c == n_gla_chunks - 1:
                finish_rows(0, 1)
            gla_apply(c)
        yield
        finish_rows(1, 2)

    tiles = [sub_tile(base) for base in range(0, ts, sub)]
    next(tiles[0])
    for i, tile in enumerate(tiles):
        next(tile)
        if i + 1 < len(tiles):
            next(tiles[i + 1])
        for _ in tile:
            pass
    for p in pairs:
        st_ref[p] = state[p]


def _layer_call(layer, x, mod, ng, win, wa2, ba, gng, lng, lnb, ws, bsp, wout, fg, *, final):
    b, s, d = x.shape
    ts = SEQ_TILE
    tiles_per_seq = s // ts
    n_tiles = b * tiles_per_seq
    xt = x.reshape(n_tiles, ts, d)
    lay3 = lambda g: (layer, 0, 0)
    lay4 = lambda g: (layer, 0, 0, 0)
    out = pl.pallas_call(
        functools.partial(_layer_kernel, final=final, tiles_per_seq=tiles_per_seq),
        grid=(n_tiles,),
        in_specs=[
            pl.BlockSpec((1, ts, d), lambda g: (g, 0, 0)),
            pl.BlockSpec((1, 1, 3, d), lambda g: (layer, g // tiles_per_seq, 0, 0)),
            pl.BlockSpec((1, 1, d), lay3),
            pl.BlockSpec((1, d, IN_WIDTH_PAD), lay3),
            pl.BlockSpec((1, RANK_PAD, GLA_KW), lay3),
            pl.BlockSpec((1, 1, GLA_KW), lay3),
            pl.BlockSpec((1, 1, GLA_WIDTH), lay3),
            pl.BlockSpec((1, 1, SG_WIDTH), lay3),
            pl.BlockSpec((1, 1, SG_WIDTH), lay3),
            pl.BlockSpec((1, SG_GROUPS, SG_CHUNK, SG_CHUNK), lay4),
            pl.BlockSpec((1, SG_CHUNK, SG_WIDTH), lay3),
            pl.BlockSpec((1, d, d), lay3),
            pl.BlockSpec((1, d), lambda g: (0, 0)),
        ],
        out_specs=pl.BlockSpec((1, ts, d), lambda g: (g, 0, 0)),
        out_shape=jax.ShapeDtypeStruct((n_tiles, ts, d), F32),
        scratch_shapes=[
            pltpu.VMEM((GLA_HEADS // 2, 2 * GLA_DK, GLA_DV), F32),
            pltpu.VMEM((ts, d), BF16),
        ],
        compiler_params=pltpu.CompilerParams(
            dimension_semantics=("arbitrary",),
            vmem_limit_bytes=VMEM_LIMIT_BYTES),
        name="gla_gmlp_layer",
    )(xt, mod, ng, win, wa2, ba, gng, lng, lnb, ws, bsp, wout, fg)
    return out.reshape(b, s, d)


def kernel(x, c, norm_g, w_ada, b_ada, w_in, w_alpha2, b_alpha, gla_norm_g, sg_ln_g, sg_ln_b,
           w_spatial, b_spatial, w_out, final_g):
    depth = w_in.shape[0]
    bsz = x.shape[0]
    d = x.shape[-1]

    mod = _ada_call(c, w_ada, b_ada).reshape(depth, bsz, 3, d)
    win = _prep_call(w_in)
    wa2 = jnp.pad(w_alpha2, ((0, 0), (0, RANK_PAD - GATE_RANK), (0, 0))).astype(BF16)
    wout = w_out.astype(BF16)
    bsp = jnp.repeat(jnp.transpose(b_spatial, (0, 2, 1)), SG_GROUP_DIM, axis=-1)
    row3 = lambda a: a[:, None, :]

    for l in range(depth):
        x = _layer_call(
            l, x, mod, row3(norm_g), win, wa2, row3(b_alpha), row3(gla_norm_g), row3(sg_ln_g),
            row3(sg_ln_b), w_spatial, bsp, wout, final_g[None], final=(l == depth - 1))
    return x
```

```python
import functools

import jax
import jax.numpy as jnp
from jax import lax
from jax.experimental import pallas as pl
from jax.experimental.pallas import tpu as pltpu

F32 = jnp.float32
BF16 = jnp.bfloat16

D_MODEL = 1024
GLA_HEADS = 4
GLA_DK = 64
GLA_DV = 128
GLA_KW = GLA_HEADS * GLA_DK
GLA_WIDTH = GLA_HEADS * GLA_DV
GATE_RANK = 16
GATE_TAU = 16.0
GLA_CHUNK = 64
SG_GROUPS = 4
SG_GROUP_DIM = 128
SG_WIDTH = SG_GROUPS * SG_GROUP_DIM
SG_CHUNK = 128
EPS = 1e-6

LANES = 128
MXU_DIM = 256
SEQ_TILE = 1024
SUB_TILE = 512
VMEM_LIMIT_BYTES = 48 * 1024 * 1024

_OFF_Q = 0
_OFF_K = _OFF_Q + GLA_KW
_OFF_A = _OFF_K + GLA_KW
_OFF_VG = _OFF_A + GLA_KW
_OFF_RG = _OFF_VG + GLA_WIDTH
_OFF_US = _OFF_RG + GLA_WIDTH
_OFF_VS = _OFF_US + SG_WIDTH
_OFF_RS = _OFF_VS + SG_WIDTH
IN_WIDTH_PAD = _OFF_RS + SG_WIDTH


def _split2(a):
    hi = a.astype(BF16)
    return hi, (a - hi.astype(F32)).astype(BF16)


def _dot(a, b):
    return jnp.dot(a, b, preferred_element_type=F32)


def _dot_nt(a, b):
    return lax.dot_general(a, b, (((1,), (1,)), ((), ())), preferred_element_type=F32)


def _dot_tn(a, b):
    return lax.dot_general(a, b, (((0,), (0,)), ((), ())), preferred_element_type=F32)


def _silu(a):
    return a * (1.0 / (1.0 + jnp.exp(-a)))


def _gelu(a):
    return 0.5 * a * (1.0 + lax.erf(a * (2.0 ** -0.5)))


def _ada_kernel(c_ref, w_ref, b_ref, o_ref):
    c = c_ref[...]
    ca = _silu(c)
    w = w_ref[0]
    a_hi, a_lo = _split2(ca)
    w_hi, w_lo = _split2(w)
    o_ref[0] = _dot(a_hi, w_hi) + (_dot(a_hi, w_lo) + _dot(a_lo, w_hi)) + b_ref[0]


def _ada_call(c, w_ada, b_ada):
    depth, d, d3 = w_ada.shape
    b = c.shape[0]
    nblk = d3 // d
    return pl.pallas_call(
        _ada_kernel,
        grid=(depth, nblk),
        in_specs=[
            pl.BlockSpec((b, d), lambda l, j: (0, 0)),
            pl.BlockSpec((1, d, d), lambda l, j: (l, 0, j)),
            pl.BlockSpec((1, 1, d), lambda l, j: (l, 0, j)),
        ],
        out_specs=pl.BlockSpec((1, b, d), lambda l, j: (l, 0, j)),
        out_shape=jax.ShapeDtypeStruct((depth, b, d3), F32),
        compiler_params=pltpu.CompilerParams(
            dimension_semantics=("arbitrary", "arbitrary"),
            vmem_limit_bytes=VMEM_LIMIT_BYTES),
        name="adaln_mod",
    )(c, w_ada, b_ada.reshape(depth, 1, d3))


PREP_K = 256


def _prep_kernel(wt_ref, wa2_ref, o_ref):
    o_v = 2 * GLA_KW
    o_a = o_v + GLA_WIDTH
    o_r = o_a + GATE_RANK
    piece = 2 * GLA_KW

    def put(dst, src, width=piece):
        o_ref[0, :, dst:dst + width] = wt_ref[0, src:src + width, :].T.astype(BF16)

    put(_OFF_Q, 0)
    a_hi, a_lo = (part.astype(F32) for part in _split2(wt_ref[0, o_a:o_r, :]))
    w_hi, w_lo = (part.astype(F32) for part in _split2(wa2_ref[0]))
    o_ref[0, :, _OFF_A:_OFF_VG] = (
        _dot_tn(a_hi, w_hi) + (_dot_tn(a_hi, w_lo) + _dot_tn(a_lo, w_hi))).astype(BF16)
    put(_OFF_VG, o_v)
    for i in range((IN_WIDTH_PAD - _OFF_RG) // piece):
        put(_OFF_RG + i * piece, o_r + i * piece)


def _prep_call(w_in, w_alpha2):
    depth, d, n_in = w_in.shape
    wt = jnp.swapaxes(w_in, 1, 2)
    return pl.pallas_call(
        _prep_kernel,
        grid=(depth, d // PREP_K),
        in_specs=[pl.BlockSpec((1, n_in, PREP_K), lambda l, i: (l, 0, i)),
                  pl.BlockSpec((1, GATE_RANK, GLA_KW), lambda l, i: (l, 0, 0))],
        out_specs=pl.BlockSpec((1, PREP_K, IN_WIDTH_PAD), lambda l, i: (l, i, 0)),
        out_shape=jax.ShapeDtypeStruct((depth, d, IN_WIDTH_PAD), BF16),
        compiler_params=pltpu.CompilerParams(
            dimension_semantics=("arbitrary", "arbitrary"),
            vmem_limit_bytes=VMEM_LIMIT_BYTES),
        name="w_in_relayout",
    )(wt, w_alpha2)


def _layer_kernel(xc_ref, modc_ref, ng_ref, win_ref, ba_ref, gng_ref,
                  lng_ref, lnb_ref, ws_ref, bsp_ref, wout_ref, fg_ref, o_ref,
                  st_ref, yin_ref, *, final, tiles_per_seq):
    g = pl.program_id(0)
    ts = xc_ref.shape[1]
    sub = min(SUB_TILE, ts)
    n_gla_chunks = sub // GLA_CHUNK
    n_sg_chunks = sub // SG_CHUNK
    half_rows = sub // 2
    pairs = range(GLA_HEADS // 2)

    @pl.when(g % tiles_per_seq == 0)
    def _():
        st_ref[...] = jnp.zeros_like(st_ref)

    gng = gng_ref[0]
    lng = lng_ref[0]
    lnb = lnb_ref[0]
    shift = modc_ref[0, 0, 0:1, :]
    scale = modc_ref[0, 0, 1:2, :]
    norm_gain = ng_ref[0] * (1.0 + scale)

    row = lax.broadcasted_iota(jnp.int32, (MXU_DIM, MXU_DIM), 0)
    col = lax.broadcasted_iota(jnp.int32, (MXU_DIM, MXU_DIM), 1)
    same_chunk = (row // GLA_CHUNK) == (col // GLA_CHUNK)
    cum_mat = jnp.where(same_chunk & (col <= row), 1.0, 0.0).astype(BF16)

    pr = lax.broadcasted_iota(jnp.int32, (2 * GLA_CHUNK, LANES), 0)
    pc = lax.broadcasted_iota(jnp.int32, (2 * GLA_CHUNK, LANES), 1)
    own_lanes = (pr // GLA_CHUNK) == (pc // GLA_DK)
    at_ = lax.broadcasted_iota(jnp.int32, (GLA_CHUNK, 2 * GLA_CHUNK), 0)
    as_ = lax.broadcasted_iota(jnp.int32, (GLA_CHUNK, 2 * GLA_CHUNK), 1)
    pair_causal = (as_ % GLA_CHUNK) <= at_

    def stack_heads(a):
        a2 = jnp.concatenate([a, a], axis=0)
        return jnp.where(own_lanes, a2, jnp.zeros_like(a2))

    st_ = lax.broadcasted_iota(jnp.int32, (SG_CHUNK, SG_CHUNK), 0)
    ss_ = lax.broadcasted_iota(jnp.int32, (SG_CHUNK, SG_CHUNK), 1)
    sg_causal = ss_ <= st_
    sg_half = SG_WIDTH // 2

    state = [st_ref[p] for p in pairs]

    def sub_tile(base):
        xx = xc_ref[0, base:base + sub, :]
        ms = jnp.mean(xx * xx, axis=-1, keepdims=True)
        hb = ((xx * lax.rsqrt(ms + EPS)) * norm_gain + shift).astype(BF16)
        yield

        def proj(off, width, rows=slice(None)):
            return _dot(hb[rows], win_ref[0, :, off:off + width])

        z = proj(_OFF_A, GLA_KW) + ba_ref[0]
        qk = proj(_OFF_Q, 2 * GLA_KW)
        q = qk[:, :GLA_KW]
        k = qk[:, GLA_KW:]
        log_a = (jnp.minimum(z, 0.0) - jnp.log(1.0 + jnp.exp(-jnp.abs(z)))) * (1.0 / GATE_TAU)
        v_g = proj(_OFF_VG, GLA_WIDTH).astype(BF16)

        la_parts = _split2(log_a)
        g_cum = jnp.concatenate(
            [sum(_dot(cum_mat, part[r:r + MXU_DIM]) for part in la_parts)
             for r in range(0, sub, MXU_DIM)], axis=0)
        r_g = _silu(proj(_OFF_RG, GLA_WIDTH))

        q_t = (q * jnp.exp(g_cum) * (GLA_DK ** -0.5)).astype(BF16)
        k_t = (k * jnp.exp(-g_cum)).astype(BF16)

        last_rows = jnp.concatenate(
            [g_cum[(c + 1) * GLA_CHUNK - 1:(c + 1) * GLA_CHUNK, :] for c in range(n_gla_chunks)],
            axis=0)
        dec_cols = jnp.exp(last_rows).T

        att = {}
        o_rows = [[None] * len(pairs) for _ in range(n_gla_chunks)]

        def gla_scores(c):
            rows = slice(c * GLA_CHUNK, (c + 1) * GLA_CHUNK)
            for p in pairs:
                lanes = slice(p * LANES, (p + 1) * LANES)
                s = _dot_nt(q_t[rows, lanes], stack_heads(k_t[rows, lanes]))
                att[c, p] = jnp.where(pair_causal, s, 0.0).astype(BF16)

        def gla_apply(c):
            rows = slice(c * GLA_CHUNK, (c + 1) * GLA_CHUNK)
            k_h = (k[rows] * jnp.exp(last_rows[c:c + 1, :] - g_cum[rows])).astype(BF16)
            for p in pairs:
                lanes = slice(p * LANES, (p + 1) * LANES)
                v0 = v_g[rows, (2 * p) * GLA_DV:(2 * p + 1) * GLA_DV]
                v1 = v_g[rows, (2 * p + 1) * GLA_DV:(2 * p + 2) * GLA_DV]
                zero = jnp.zeros_like(v0)
                s_b = state[p].astype(BF16)
                rhs = jnp.concatenate(
                    [jnp.concatenate([v0, zero], axis=1),
                     jnp.concatenate([zero, v1], axis=1),
                     jnp.concatenate([s_b[:GLA_DK], zero], axis=1),
                     jnp.concatenate([zero, s_b[GLA_DK:]], axis=1)], axis=0)
                lhs = jnp.concatenate([att.pop((c, p)), q_t[rows, lanes]], axis=1)
                o_rows[c][p] = _dot(lhs, rhs)
                state[p] = (dec_cols[lanes, c:c + 1] * state[p]
                            + _dot_tn(stack_heads(k_h[:, lanes]),
                                      jnp.concatenate([v0, v1], axis=0)))

        acts = {}

        def sg_proj(name, off, act, hf):
            acts[name, hf] = act(proj(off + hf * sg_half, sg_half))

        def sg_group(grp):
            hf, sl = divmod(grp * SG_GROUP_DIM, sg_half)
            sl = slice(sl, sl + SG_GROUP_DIM)
            gl = slice(grp * SG_GROUP_DIM, (grp + 1) * SG_GROUP_DIM)
            v_grp = acts["v", hf][:, sl]
            mu = jnp.mean(v_grp, axis=-1, keepdims=True)
            cen = v_grp - mu
            var = jnp.mean(cen * cen, axis=-1, keepdims=True)
            vn = (cen * lax.rsqrt(var + EPS) * lng[:, gl] + lnb[:, gl]).astype(BF16)
            w_g = jnp.where(sg_causal, ws_ref[0, grp], 0.0).astype(BF16)
            vn_wide = jnp.concatenate(
                [vn[n * SG_CHUNK:(n + 1) * SG_CHUNK, :] for n in range(n_sg_chunks)], axis=1)
            mixed_wide = _dot(w_g, vn_wide)
            bias = bsp_ref[0, :, gl]
            for n in range(n_sg_chunks):
                rs_ = slice(n * SG_CHUNK, (n + 1) * SG_CHUNK)
                mixed = mixed_wide[:, n * SG_GROUP_DIM:(n + 1) * SG_GROUP_DIM] + bias
                out = acts["u", hf][rs_, sl] * mixed * acts["r", hf][rs_, sl]
                yin_ref[base + n * SG_CHUNK:base + (n + 1) * SG_CHUNK,
                        GLA_WIDTH + grp * SG_GROUP_DIM:GLA_WIDTH + (grp + 1) * SG_GROUP_DIM] = (
                    out.astype(BF16))

        def finish_rows(rh, n_split):
            rows = slice(rh * half_rows, (rh + 1) * half_rows)
            out_rows = slice(base + rh * half_rows, base + (rh + 1) * half_rows)
            chunks = range(rh * n_gla_chunks // 2, (rh + 1) * n_gla_chunks // 2)
            for hd in range(GLA_HEADS):
                sl = slice(hd * GLA_DV, (hd + 1) * GLA_DV)
                o_h = jnp.concatenate(
                    [o_rows[c][hd // 2][:, (hd % 2) * GLA_DV:(hd % 2 + 1) * GLA_DV]
                     for c in chunks], axis=0)
                o_h = (o_h * lax.rsqrt(jnp.mean(o_h * o_h, axis=-1, keepdims=True) + EPS)
                       * gng[:, sl])
                yin_ref[out_rows, sl] = (o_h * r_g[rows, sl]).astype(BF16)
            width = D_MODEL // n_split
            parts = []
            for nh in range(n_split):
                cols = slice(nh * width, (nh + 1) * width)
                y = _dot(yin_ref[out_rows, :], wout_ref[0, :, cols])
                parts.append(xc_ref[0, out_rows, cols] + modc_ref[0, 0, 2:3, cols] * y)
            if final:
                ms2 = sum(jnp.sum(part * part, axis=-1, keepdims=True) for part in parts) / D_MODEL
                scale_rows = lax.rsqrt(ms2 + EPS)
                parts = [part * scale_rows * fg_ref[:, nh * width:(nh + 1) * width]
                         for nh, part in enumerate(parts)]
            for nh, part in enumerate(parts):
                o_ref[0, out_rows, nh * width:(nh + 1) * width] = part

        fillers = [
            functools.partial(sg_proj, "v", _OFF_VS, _gelu, 0),
            functools.partial(sg_proj, "u", _OFF_US, _gelu, 0),
            functools.partial(sg_proj, "r", _OFF_RS, _silu, 0),
            functools.partial(sg_proj, "v", _OFF_VS, _gelu, 1),
            lambda: (sg_group(0), sg_group(1), sg_proj("u", _OFF_US, _gelu, 1)),
            functools.partial(sg_proj, "r", _OFF_RS, _silu, 1),
            lambda: (sg_group(2), sg_group(3)),
        ]
        gla_scores(0)
        for c in range(n_gla_chunks):
            if c + 1 < n_gla_chunks:
                gla_scores(c + 1)
            for _ in range(-(-len(fillers) // (n_gla_chunks - c))):
                fillers.pop(0)()
            if c == n_gla_chunks - 1:
                finish_rows(0, 1)
            gla_apply(c)
        yield
        finish_rows(1, 2)

    tiles = [sub_tile(base) for base in range(0, ts, sub)]
    next(tiles[0])
    for i, tile in enumerate(tiles):
        next(tile)
        if i + 1 < len(tiles):
            next(tiles[i + 1])
        for _ in tile:
            pass
    for p in pairs:
        st_ref[p] = state[p]


def _layer_call(layer, x, mod, ng, win, ba, gng, lng, lnb, ws, bsp, wout, fg, *, final):
    b, s, d = x.shape
    ts = SEQ_TILE
    tiles_per_seq = s // ts
    n_tiles = b * tiles_per_seq
    xt = x.reshape(n_tiles, ts, d)
    lay3 = lambda g: (layer, 0, 0)
    lay4 = lambda g: (layer, 0, 0, 0)
    out = pl.pallas_call(
        functools.partial(_layer_kernel, final=final, tiles_per_seq=tiles_per_seq),
        grid=(n_tiles,),
        in_specs=[
            pl.BlockSpec((1, ts, d), lambda g: (g, 0, 0)),
            pl.BlockSpec((1, 1, 3, d), lambda g: (layer, g // tiles_per_seq, 0, 0)),
            pl.BlockSpec((1, 1, d), lay3),
            pl.BlockSpec((1, d, IN_WIDTH_PAD), lay3),
            pl.BlockSpec((1, 1, GLA_KW), lay3),
            pl.BlockSpec((1, 1, GLA_WIDTH), lay3),
            pl.BlockSpec((1, 1, SG_WIDTH), lay3),
            pl.BlockSpec((1, 1, SG_WIDTH), lay3),
            pl.BlockSpec((1, SG_GROUPS, SG_CHUNK, SG_CHUNK), lay4),
            pl.BlockSpec((1, SG_CHUNK, SG_WIDTH), lay3),
            pl.BlockSpec((1, d, d), lay3),
            pl.BlockSpec((1, d), lambda g: (0, 0)),
        ],
        out_specs=pl.BlockSpec((1, ts, d), lambda g: (g, 0, 0)),
        out_shape=jax.ShapeDtypeStruct((n_tiles, ts, d), F32),
        scratch_shapes=[
            pltpu.VMEM((GLA_HEADS // 2, 2 * GLA_DK, GLA_DV), F32),
            pltpu.VMEM((ts, d), BF16),
        ],
        compiler_params=pltpu.CompilerParams(
            dimension_semantics=("arbitrary",),
            vmem_limit_bytes=VMEM_LIMIT_BYTES),
        name="gla_gmlp_layer",
    )(xt, mod, ng, win, ba, gng, lng, lnb, ws, bsp, wout, fg)
    return out.reshape(b, s, d)


def kernel(x, c, norm_g, w_ada, b_ada, w_in, w_alpha2, b_alpha, gla_norm_g, sg_ln_g, sg_ln_b,
           w_spatial, b_spatial, w_out, final_g):
    depth = w_in.shape[0]
    bsz = x.shape[0]
    d = x.shape[-1]

    mod = _ada_call(c, w_ada, b_ada).reshape(depth, bsz, 3, d)
    win = _prep_call(w_in, w_alpha2)
    wout = w_out.astype(BF16)
    bsp = jnp.repeat(jnp.transpose(b_spatial, (0, 2, 1)), SG_GROUP_DIM, axis=-1)
    row3 = lambda a: a[:, None, :]

    for l in range(depth):
        x = _layer_call(
            l, x, mod, row3(norm_g), win, row3(b_alpha), row3(gla_norm_g), row3(sg_ln_g),
            row3(sg_ln_b), w_spatial, bsp, wout, final_g[None], final=(l == depth - 1))
    return x
```
